```python
import math
import jax, jax.numpy as jnp
from jax import lax
import numpy as np

D_MODEL = 1024
BATCH = 8
SEQ = 2048
DEPTH = 1

MEM_LEN = 256
EPS = 1e-6
RET_HEADS = 4
RET_DK = 128
RET_DV = 256
CHUNK = 128
ROPE_BASE = 10000.0
Q_W = RET_HEADS * RET_DK
V_W = RET_HEADS * RET_DV
LRU_WIDTH = 1024
LRU_BLOCKS = 8
LRU_BLOCK = LRU_WIDTH // LRU_BLOCKS
CONV_W = 4
LRU_C = 8.0
IN_W = 2 * Q_W + 2 * V_W + 2 * LRU_WIDTH
N_BRANCH = 2
D_FF = 2816
X_HEADS = 4
X_HD = D_MODEL // X_HEADS

kernel_name = "hybrid_retention_rglru_macaron_xattn"


def rmsnorm(x, g):
    xf = x.astype(jnp.float32)
    y = xf * lax.rsqrt(jnp.mean(xf * xf, axis=-1, keepdims=True) + EPS)
    return (y * g.astype(jnp.float32)).astype(x.dtype)


def swiglu(h, w1, w3, w2):
    return (jax.nn.silu(h @ w1) * (h @ w3)) @ w2


def rotary(t, pos):
    dk = t.shape[-1]
    inv_freq = ROPE_BASE ** (-jnp.arange(0, dk, 2, dtype=jnp.float32) / dk)
    ang = pos[:, None] * inv_freq[None, :]
    cos = jnp.cos(ang)[None, :, None, :]
    sin = jnp.sin(ang)[None, :, None, :]
    t1, t2 = t[..., : dk // 2], t[..., dk // 2:]
    return jnp.concatenate([t1 * cos - t2 * sin, t2 * cos + t1 * sin], axis=-1)


def retention_chunkwise(q, k, v):
    B, S, H, dk = q.shape
    dv = v.shape[-1]
    nC = S // CHUNK
    log_gamma = jnp.log(1.0 - 2.0 ** (-5.0 - jnp.arange(H, dtype=jnp.float32)))
    q = q.reshape(B, nC, CHUNK, H, dk)
    k = k.reshape(B, nC, CHUNK, H, dk) * (dk ** -0.5)
    v = v.reshape(B, nC, CHUNK, H, dv)
    pos = jnp.arange(CHUNK, dtype=jnp.float32)
    rel = pos[:, None] - pos[None, :]
    decay = jnp.where(rel[None] >= 0, jnp.exp(rel[None] * log_gamma[:, None, None]), 0.0)
    scores = jnp.einsum('bnihd,bnjhd->bnhij', q, k) * decay[None, None]
    inner = jnp.einsum('bnhij,bnjhe->bnihe', scores, v)
    k_decay = jnp.exp((CHUNK - 1.0 - pos)[None, :] * log_gamma[:, None])
    kv = jnp.einsum('bnjhd,bnjhe,hj->nbhde', k, v, k_decay)
    chunk_decay = jnp.exp(CHUNK * log_gamma)[None, :, None, None]

    def step(state, kv_c):
        return chunk_decay * state + kv_c, state

    init = jnp.zeros((B, H, dk, dv), jnp.float32)
    _, prev = lax.scan(step, init, kv)
    prev = jnp.moveaxis(prev, 0, 1)
    q_decay = jnp.exp((pos + 1.0)[:, None] * log_gamma[None, :])
    cross = jnp.einsum('bnihd,bnhde->bnihe', q, prev) * q_decay[None, None, :, :, None]
    return (inner + cross).reshape(B, S, H, dv)


def head_groupnorm(y, g):
    mu = jnp.mean(y, axis=-1, keepdims=True)
    var = jnp.mean(jnp.square(y - mu), axis=-1, keepdims=True)
    yn = (y - mu) * lax.rsqrt(var + EPS)
    B, S, H, dv = y.shape
    return yn.reshape(B, S, H * dv) * g.astype(jnp.float32)


def causal_depthwise_conv(x, w, b):
    S = x.shape[1]
    xp = jnp.pad(x, ((0, 0), (CONV_W - 1, 0), (0, 0)))
    y = xp[:, 0:S] * w[0]
    for tap in range(1, CONV_W):
        y = y + xp[:, tap:tap + S] * w[tap]
    return y + b


def rg_lru(x, w_r, b_r, w_i, b_i, lam):
    B, S, W = x.shape
    xb = x.reshape(B, S, LRU_BLOCKS, LRU_BLOCK)
    r = jax.nn.sigmoid(jnp.einsum('bsgi,gij->bsgj', xb, w_r).reshape(B, S, W) + b_r)
    i = jax.nn.sigmoid(jnp.einsum('bsgi,gij->bsgj', xb, w_i).reshape(B, S, W) + b_i)
    log_a = -LRU_C * r * jax.nn.softplus(-lam)
    a = jnp.exp(log_a)
    mult = jnp.sqrt(-jnp.expm1(2.0 * log_a))
    bx = mult * (i * x)

    def combine(c1, c2):
        a1, b1 = c1
        a2, b2 = c2
        return a1 * a2, a2 * b1 + b2

    _, h = lax.associative_scan(combine, (a, bx), axis=1)
    return h


def setup_inputs(seed: int = 0) -> dict:
    key = jax.random.key(seed)
    ks = iter(jax.random.split(key, 64))

    def nrm(shape, scale):
        return jax.random.normal(next(ks), shape, jnp.float32) * scale

    def gain(shape):
        return 1.0 + nrm(shape, 0.02)

    L, D = DEPTH, D_MODEL
    lam_a = jax.random.uniform(next(ks), (L, LRU_WIDTH), jnp.float32, 0.9, 0.999)
    return {
        "x": nrm((BATCH, SEQ, D), 1.0),
        "mem": nrm((BATCH, MEM_LEN, D), 1.0),
        "ffn1_norm": gain((L, D)),
        "ffn1_w1": nrm((L, D, D_FF), D ** -0.5),
        "ffn1_w3": nrm((L, D, D_FF), D ** -0.5),
        "ffn1_w2": nrm((L, D_FF, D), D_FF ** -0.5),
        "mix_norm": gain((L, D)),
        "w_in": nrm((L, D, IN_W), D ** -0.5),
        "ret_gn": gain((L, V_W)),
        "w_ret_o": nrm((L, V_W, D), V_W ** -0.5),
        "conv_w": nrm((L, CONV_W, LRU_WIDTH), CONV_W ** -0.5),
        "conv_b": nrm((L, LRU_WIDTH), 0.01),
        "w_rgate": nrm((L, LRU_BLOCKS, LRU_BLOCK, LRU_BLOCK), LRU_BLOCK ** -0.5),
        "b_rgate": nrm((L, LRU_WIDTH), 0.01),
        "w_igate": nrm((L, LRU_BLOCKS, LRU_BLOCK, LRU_BLOCK), LRU_BLOCK ** -0.5),
        "b_igate": nrm((L, LRU_WIDTH), 0.01),
        "lru_lambda": jnp.log(lam_a) - jnp.log1p(-lam_a),
        "w_lru_o": nrm((L, LRU_WIDTH, D), LRU_WIDTH ** -0.5),
        "w_branch_gate": nrm((L, D, N_BRANCH * D), D ** -0.5),
        "b_branch_gate": nrm((L, N_BRANCH * D), 0.01),
        "w_out": nrm((L, D, D), D ** -0.5),
        "xattn_norm": gain((L, D)),
        "mem_norm": gain((L, D)),
        "w_xq": nrm((L, D, D), D ** -0.5),
        "w_xk": nrm((L, D, D), D ** -0.5),
        "w_xv": nrm((L, D, D), D ** -0.5),
        "w_xo": nrm((L, D, D), D ** -0.5),
        "ffn2_norm": gain((L, D)),
        "ffn2_w1": nrm((L, D, D_FF), D ** -0.5),
        "ffn2_w3": nrm((L, D, D_FF), D ** -0.5),
        "ffn2_w2": nrm((L, D_FF, D), D_FF ** -0.5),
        "final_norm": gain((D,)),
    }


def reference(x, mem, ffn1_norm, ffn1_w1, ffn1_w3, ffn1_w2, mix_norm, w_in, ret_gn, w_ret_o,
              conv_w, conv_b, w_rgate, b_rgate, w_igate, b_igate, lru_lambda, w_lru_o,
              w_branch_gate, b_branch_gate, w_out, xattn_norm, mem_norm, w_xq, w_xk, w_xv, w_xo,
              ffn2_norm, ffn2_w1, ffn2_w3, ffn2_w2, final_norm):
    B, S, D = x.shape
    M = mem.shape[1]
    pos = jnp.arange(S, dtype=jnp.float32)
    for l in range(DEPTH):
        x = x + 0.5 * swiglu(rmsnorm(x, ffn1_norm[l]), ffn1_w1[l], ffn1_w3[l], ffn1_w2[l])

        h = rmsnorm(x, mix_norm[l])
        u = h @ w_in[l]
        o1 = Q_W
        o2 = o1 + Q_W
        o3 = o2 + V_W
        o4 = o3 + V_W
        o5 = o4 + LRU_WIDTH
        q = u[..., :o1].reshape(B, S, RET_HEADS, RET_DK).astype(jnp.float32)
        k = u[..., o1:o2].reshape(B, S, RET_HEADS, RET_DK).astype(jnp.float32)
        v = u[..., o2:o3].reshape(B, S, RET_HEADS, RET_DV).astype(jnp.float32)
        g_ret = u[..., o3:o4]
        x_lru = u[..., o4:o5]
        g_lru = u[..., o5:]

        ret = retention_chunkwise(rotary(q, pos), rotary(k, pos), v)
        ret = head_groupnorm(ret, ret_gn[l])
        y_ret = (jax.nn.silu(g_ret.astype(jnp.float32)) * ret).astype(x.dtype) @ w_ret_o[l]

        xc = causal_depthwise_conv(x_lru, conv_w[l], conv_b[l]).astype(jnp.float32)
        hl = rg_lru(xc, w_rgate[l].astype(jnp.float32), b_rgate[l].astype(jnp.float32),
                    w_igate[l].astype(jnp.float32), b_igate[l].astype(jnp.float32),
                    lru_lambda[l].astype(jnp.float32))
        y_lru = (hl * jax.nn.gelu(g_lru.astype(jnp.float32))).astype(x.dtype) @ w_lru_o[l]

        gates = jax.nn.sigmoid(h @ w_branch_gate[l] + b_branch_gate[l])
        merged = gates[..., :D] * y_ret + gates[..., D:] * y_lru
        x = x + merged @ w_out[l]

        hq = rmsnorm(x, xattn_norm[l])
        m = rmsnorm(mem, mem_norm[l])
        xq = (hq @ w_xq[l]).reshape(B, S, X_HEADS, X_HD)
        xk = (m @ w_xk[l]).reshape(B, M, X_HEADS, X_HD)
        xv = (m @ w_xv[l]).reshape(B, M, X_HEADS, X_HD)
        sc = jnp.einsum('bshd,bmhd->bhsm', xq.astype(jnp.float32), xk.astype(jnp.float32)) * (X_HD ** -0.5)
        p = jax.nn.softmax(sc, axis=-1)
        xo = jnp.einsum('bhsm,bmhd->bshd', p, xv.astype(jnp.float32)).reshape(B, S, D).astype(x.dtype)
        x = x + xo @ w_xo[l]

        x = x + 0.5 * swiglu(rmsnorm(x, ffn2_norm[l]), ffn2_w1[l], ffn2_w3[l], ffn2_w2[l])
    return rmsnorm(x, final_norm)
```

```python
import functools
import math

import jax
import jax.numpy as jnp
from jax import lax
from jax.experimental import pallas as pl
from jax.experimental.pallas import tpu as pltpu

EPS = 1e-6
RET_HEADS = 4
RET_DK = 128
RET_DV = 256
RET_CHUNK = 128
ROPE_BASE = 10000.0
LRU_BLOCKS = 8
CONV_W = 4
LRU_C = 8.0
X_HEADS = 4

SUBLANES = 8
VMEM_LIMIT_BYTES = 56 * 1024 * 1024

BF16 = jnp.bfloat16
F32 = jnp.float32


def _dot(a, b):
    return jnp.dot(a, b, preferred_element_type=F32)


def _rms(x, g):
    ms = jnp.mean(x * x, axis=-1, keepdims=True)
    return x * lax.rsqrt(ms + EPS) * g


def _const_spec(shape):
    nd = len(shape)
    return pl.BlockSpec(shape, lambda *_: (0,) * nd, pipeline_mode=pl.Buffered(1))


def _swiglu_half_step(x, g, w1_ref, w3_ref, w2_ref, ff_chunk):
    h = _rms(x, g).astype(BF16)
    d_ff = w1_ref.shape[1]
    acc = None
    for j in range(d_ff // ff_chunk):
        cols = slice(j * ff_chunk, (j + 1) * ff_chunk)
        a = _dot(h, w1_ref[:, cols])
        b = _dot(h, w3_ref[:, cols])
        hid = (a * jax.nn.sigmoid(a) * b).astype(BF16)
        part = _dot(hid, w2_ref[cols, :])
        acc = part if acc is None else acc + part
    return x + 0.5 * acc


def _ffn_kernel(x_ref, g_ref, w1_ref, w3_ref, w2_ref, o_ref, *, ff_chunk):
    o_ref[...] = _swiglu_half_step(x_ref[...], g_ref[...], w1_ref, w3_ref, w2_ref, ff_chunk)


def _ffn_call(x2d, g, w1, w3, w2, *, tm, ff_chunk):
    t, d = x2d.shape
    d_ff = w1.shape[1]
    return pl.pallas_call(
        functools.partial(_ffn_kernel, ff_chunk=ff_chunk),
        out_shape=jax.ShapeDtypeStruct((t, d), F32),
        grid=(t // tm,),
        in_specs=[
            pl.BlockSpec((tm, d), lambda i: (i, 0)),
            _const_spec((1, d)),
            _const_spec((d, d_ff)),
            _const_spec((d, d_ff)),
            _const_spec((d_ff, d)),
        ],
        out_specs=pl.BlockSpec((tm, d), lambda i: (i, 0)),
        compiler_params=pltpu.CompilerParams(
            dimension_semantics=("arbitrary",), vmem_limit_bytes=VMEM_LIMIT_BYTES),
        name="ffn1",
    )(x2d, g, w1, w3, w2)


def _mixer_kernel(x_ref, g_ref, w_in_ref, w_bg_ref, b_bg_ref,
                  cosq_ref, sinq_ref, dmat_ref, qdec_ref, kdec_ref, cdec_ref,
                  gn_ref, w_ro_ref,
                  cw_ref, cb_ref, w_ri_ref, b_r_ref, b_i_ref, lam_ref, w_lo_ref,
                  w_out_ref, o_ref,
                  state_ref, hcar_ref, cbuf_ref, gr_ref, *, ts):
    d = x_ref.shape[-1]
    qw = RET_HEADS * RET_DK
    vw = RET_HEADS * RET_DV
    lw = lam_ref.shape[-1]
    lb = lw // LRU_BLOCKS
    o_k, o_v, o_g, o_x, o_gl = qw, 2 * qw, 2 * qw + vw, 2 * qw + 2 * vw, 2 * qw + 2 * vw + lw

    @pl.when(pl.program_id(1) == 0)
    def _():
        state_ref[...] = jnp.zeros_like(state_ref)
        hcar_ref[...] = jnp.zeros_like(hcar_ref)
        cbuf_ref[0:SUBLANES, :] = jnp.zeros((SUBLANES, lw), F32)

    x = x_ref[0]
    h = _rms(x, g_ref[...]).astype(BF16)

    q = _dot(h, w_in_ref[:, 0:o_k])
    k = _dot(h, w_in_ref[:, o_k:o_v])
    v = _dot(h, w_in_ref[:, o_v:o_g]).astype(BF16)
    g_ret = _dot(h, w_in_ref[:, o_g:o_x])
    cosq = cosq_ref[...]
    sinq = sinq_ref[...]
    k_scale = RET_DK ** -0.5
    for hd in range(RET_HEADS):
        qs = slice(hd * RET_DK, (hd + 1) * RET_DK)
        vs = slice(hd * RET_DV, (hd + 1) * RET_DV)
        qh = q[:, qs]
        kh = k[:, qs]
        qh = qh * cosq + pltpu.roll(qh, RET_DK // 2, axis=1) * sinq
        kh = (kh * cosq + pltpu.roll(kh, RET_DK // 2, axis=1) * sinq) * k_scale
        dmat = dmat_ref[hd]
        qdec = qdec_ref[hd]
        kdec = kdec_ref[hd]
        cdec = cdec_ref[hd]
        for c in range(ts // RET_CHUNK):
            rows = slice(c * RET_CHUNK, (c + 1) * RET_CHUNK)
            qc = qh[rows]
            kc = kh[rows]
            vc = v[rows, vs]
            st = state_ref[hd]
            scores = lax.dot_general(qc.astype(BF16), kc.astype(BF16),
                                     (((1,), (1,)), ((), ())),
                                     preferred_element_type=F32) * dmat
            inner = _dot(scores.astype(BF16), vc)
            cross = _dot((qc * qdec).astype(BF16), st.astype(BF16))
            kv = lax.dot_general((kc * kdec).astype(BF16), vc,
                                 (((0,), (0,)), ((), ())),
                                 preferred_element_type=F32)
            state_ref[hd] = cdec * st + kv
            y = inner + cross
            mu = jnp.mean(y, axis=-1, keepdims=True)
            yc = y - mu
            var = jnp.mean(yc * yc, axis=-1, keepdims=True)
            yn = yc * lax.rsqrt(var + EPS) * gn_ref[:, vs]
            gt = g_ret[rows, vs]
            gr_ref[rows, vs] = (gt * jax.nn.sigmoid(gt) * yn).astype(BF16)
    y_ret = _dot(gr_ref[...], w_ro_ref[...])

    x_lru = _dot(h, w_in_ref[:, o_x:o_gl])
    cbuf_ref[SUBLANES:, :] = x_lru
    xc = x_lru * cw_ref[CONV_W - 1:CONV_W, :] + cb_ref[...]
    for tap in range(CONV_W - 1):
        back = CONV_W - 1 - tap
        xc = xc + cbuf_ref[SUBLANES - back:SUBLANES - back + ts, :] * cw_ref[tap:tap + 1, :]
    cbuf_ref[0:SUBLANES, :] = cbuf_ref[ts:ts + SUBLANES, :]

    xcb = xc.astype(BF16)
    ri = [_dot(xcb[:, gb * lb:(gb + 1) * lb], w_ri_ref[gb]) for gb in range(LRU_BLOCKS)]
    r = jax.nn.sigmoid(jnp.concatenate([p[:, :lb] for p in ri], axis=1) + b_r_ref[...])
    ig = jax.nn.sigmoid(jnp.concatenate([p[:, lb:] for p in ri], axis=1) + b_i_ref[...])
    nlam = -lam_ref[...]
    softplus = jnp.maximum(nlam, 0.0) + jnp.log1p(jnp.exp(-jnp.abs(nlam)))
    log_a = (-LRU_C * softplus) * r
    a = jnp.exp(log_a)
    th = jnp.tanh(log_a)
    mult = jnp.sqrt(-2.0 * th / (1.0 - th))
    bx = mult * (ig * xc)

    row = lax.broadcasted_iota(jnp.int32, (ts, lw), 0)
    sh = 1
    while sh < ts:
        keep = row >= sh
        a_prev = jnp.where(keep, pltpu.roll(a, sh, axis=0), 1.0)
        b_prev = jnp.where(keep, pltpu.roll(bx, sh, axis=0), 0.0)
        bx = a * b_prev + bx
        a = a * a_prev
        sh *= 2
    hl = a * hcar_ref[...] + bx
    hcar_ref[...] = hl[ts - 1:ts, :]

    g_lru = _dot(h, w_in_ref[:, o_gl:])
    y_lru = _dot((hl * jax.nn.gelu(g_lru)).astype(BF16), w_lo_ref[...])

    gates = jax.nn.sigmoid(_dot(h, w_bg_ref[...]) + b_bg_ref[...])
    merged = gates[:, :d] * y_ret + gates[:, d:] * y_lru
    o_ref[0] = x + _dot(merged.astype(BF16), w_out_ref[...])


def _mixer_call(x, g, w_in, w_bg, b_bg, cosq, sinq, dmat, qdec, kdec, cdec, gn, w_ro,
                cw, cb, w_ri, b_r, b_i, lam, w_lo, w_out, *, ts):
    b, s, d = x.shape
    lw = lam.shape[-1]
    consts = (g, w_in, w_bg, b_bg)
    consts2 = (dmat, qdec, kdec, cdec, gn, w_ro, cw, cb, w_ri, b_r, b_i, lam, w_lo, w_out)
    rope_spec = pl.BlockSpec((ts, RET_DK), lambda bi, si: (si, 0))
    return pl.pallas_call(
        functools.partial(_mixer_kernel, ts=ts),
        out_shape=jax.ShapeDtypeStruct((b, s, d), F32),
        grid=(b, s // ts),
        in_specs=([pl.BlockSpec((1, ts, d), lambda bi, si: (bi, si, 0))]
                  + [_const_spec(c.shape) for c in consts]
                  + [rope_spec, rope_spec]
                  + [_const_spec(c.shape) for c in consts2]),
        out_specs=pl.BlockSpec((1, ts, d), lambda bi, si: (bi, si, 0)),
        scratch_shapes=[
            pltpu.VMEM((RET_HEADS, RET_DK, RET_DV), F32),
            pltpu.VMEM((1, lw), F32),
            pltpu.VMEM((ts + SUBLANES, lw), F32),
            pltpu.VMEM((ts, RET_HEADS * RET_DV), BF16),
        ],
        compiler_params=pltpu.CompilerParams(
            dimension_semantics=("arbitrary", "arbitrary"), vmem_limit_bytes=VMEM_LIMIT_BYTES),
        name="mixer",
    )(x, *consts, cosq, sinq, *consts2)


def _memkv_kernel(m_ref, g_ref, wk_ref, wv_ref, k_ref, v_ref):
    m = _rms(m_ref[...], g_ref[...]).astype(BF16)
    k_ref[...] = _dot(m, wk_ref[...]).astype(BF16)
    v_ref[...] = _dot(m, wv_ref[...]).astype(BF16)


def _memkv_call(mem2d, g, wk, wv, *, tm):
    t, d = mem2d.shape
    return pl.pallas_call(
        _memkv_kernel,
        out_shape=(jax.ShapeDtypeStruct((t, d), BF16), jax.ShapeDtypeStruct((t, d), BF16)),
        grid=(t // tm,),
        in_specs=[pl.BlockSpec((tm, d), lambda i: (i, 0)), _const_spec((1, d)),
                  _const_spec((d, d)), _const_spec((d, d))],
        out_specs=(pl.BlockSpec((tm, d), lambda i: (i, 0)), pl.BlockSpec((tm, d), lambda i: (i, 0))),
        compiler_params=pltpu.CompilerParams(
            dimension_semantics=("arbitrary",), vmem_limit_bytes=VMEM_LIMIT_BYTES),
        name="mem_kv",
    )(mem2d, g, wk, wv)


def _xattn_ffn_kernel(x_ref, gq_ref, wq_ref, k_ref, v_ref, wo_ref,
                      g2_ref, w1_ref, w3_ref, w2_ref, gf_ref, o_ref, xo_ref, *, ff_chunk):
    x = x_ref[0]
    d = x.shape[-1]
    hd = d // X_HEADS
    hq = _rms(x, gq_ref[...]).astype(BF16)
    xq = (_dot(hq, wq_ref[...]) * (hd ** -0.5)).astype(BF16)
    for hh in range(X_HEADS):
        cs = slice(hh * hd, (hh + 1) * hd)
        sc = lax.dot_general(xq[:, cs], k_ref[0, :, cs], (((1,), (1,)), ((), ())),
                             preferred_element_type=F32)
        e = jnp.exp(sc - jnp.max(sc, axis=-1, keepdims=True))
        p = e / jnp.sum(e, axis=-1, keepdims=True)
        xo_ref[:, cs] = _dot(p.astype(BF16), v_ref[0, :, cs]).astype(BF16)
    x3 = x + _dot(xo_ref[...], wo_ref[...])
    y = _swiglu_half_step(x3, g2_ref[...], w1_ref, w3_ref, w2_ref, ff_chunk)
    o_ref[0] = _rms(y, gf_ref[...])


def _xattn_ffn_call(x, gq, wq, xk, xv, wo, g2, w1, w3, w2, gf, *, ts, ff_chunk):
    b, s, d = x.shape
    m = xk.shape[1]
    d_ff = w1.shape[1]
    return pl.pallas_call(
        functools.partial(_xattn_ffn_kernel, ff_chunk=ff_chunk),
        out_shape=jax.ShapeDtypeStruct((b, s, d), F32),
        grid=(b, s // ts),
        in_specs=[
            pl.BlockSpec((1, ts, d), lambda bi, si: (bi, si, 0)),
            _const_spec((1, d)),
            _const_spec((d, d)),
            pl.BlockSpec((1, m, d), lambda bi, si: (bi, 0, 0)),
            pl.BlockSpec((1, m, d), lambda bi, si: (bi, 0, 0)),
            _const_spec((d, d)),
            _const_spec((1, d)),
            _const_spec((d, d_ff)),
            _const_spec((d, d_ff)),
            _const_spec((d_ff, d)),
            _const_spec((1, d)),
        ],
        out_specs=pl.BlockSpec((1, ts, d), lambda bi, si: (bi, si, 0)),
        scratch_shapes=[pltpu.VMEM((ts, d), BF16)],
        compiler_params=pltpu.CompilerParams(
            dimension_semantics=("arbitrary", "arbitrary"), vmem_limit_bytes=VMEM_LIMIT_BYTES),
        name="xattn_ffn2",
    )(x, gq, wq, xk, xv, wo, g2, w1, w3, w2, gf)


def _rope_tables(s):
    pos = jnp.arange(s, dtype=F32)
    inv_freq = ROPE_BASE ** (-jnp.arange(0, RET_DK, 2, dtype=F32) / RET_DK)
    ang = pos[:, None] * inv_freq[None, :]
    cos, sin = jnp.cos(ang), jnp.sin(ang)
    return jnp.concatenate([cos, cos], axis=-1), jnp.concatenate([-sin, sin], axis=-1)


def _decay_tables():
    log_gamma = jnp.log(1.0 - 2.0 ** (-5.0 - jnp.arange(RET_HEADS, dtype=F32)))
    lg = log_gamma[:, None, None]
    pos = jnp.arange(RET_CHUNK, dtype=F32)
    rel = pos[:, None] - pos[None, :]
    dmat = jnp.where(rel[None] >= 0, jnp.exp(rel[None] * lg), 0.0)
    ones = jnp.ones((1, RET_CHUNK, RET_DK), F32)
    qdec = jnp.exp((pos + 1.0)[None, :, None] * lg) * ones
    kdec = jnp.exp((RET_CHUNK - 1.0 - pos)[None, :, None] * lg) * ones
    cdec = jnp.exp(RET_CHUNK * lg) * jnp.ones((1, RET_DK, RET_DV), F32)
    return dmat, qdec, kdec, cdec


def kernel(x, mem, ffn1_norm, ffn1_w1, ffn1_w3, ffn1_w2, mix_norm, w_in, ret_gn, w_ret_o, conv_w, conv_b, w_rgate, b_rgate, w_igate, b_igate, lru_lambda, w_lru_o, w_branch_gate, b_branch_gate, w_out, xattn_norm, mem_norm, w_xq, w_xk, w_xv, w_xo, ffn2_norm, ffn2_w1, ffn2_w3, ffn2_w2, final_norm):
    b, s, d = x.shape
    m = mem.shape[1]
    depth = ffn1_norm.shape[0]
    tm, ts, ff_chunk = 512, 256, 256
    bf = lambda w: w.astype(BF16)
    row = lambda p: p.reshape(1, -1)
    cosq, sinq = _rope_tables(s)
    dmat, qdec, kdec, cdec = _decay_tables()
    for l in range(depth):
        x = _ffn_call(x.reshape(b * s, d), row(ffn1_norm[l]), bf(ffn1_w1[l]), bf(ffn1_w3[l]),
                      bf(ffn1_w2[l]), tm=tm, ff_chunk=ff_chunk).reshape(b, s, d)
        w_ri = bf(jnp.concatenate([w_rgate[l], w_igate[l]], axis=-1))
        x = _mixer_call(x, row(mix_norm[l]), bf(w_in[l]), bf(w_branch_gate[l]), row(b_branch_gate[l]),
                        cosq, sinq, dmat, qdec, kdec, cdec, row(ret_gn[l]), bf(w_ret_o[l]),
                        conv_w[l], row(conv_b[l]), w_ri, row(b_rgate[l]), row(b_igate[l]),
                        row(lru_lambda[l]), bf(w_lru_o[l]), bf(w_out[l]), ts=ts)
        xk, xv = _memkv_call(mem.reshape(b * m, d), row(mem_norm[l]), bf(w_xk[l]), bf(w_xv[l]), tm=m)
        assert depth == 1
        x = _xattn_ffn_call(x, row(xattn_norm[l]), bf(w_xq[l]), xk.reshape(b, m, d), xv.reshape(b, m, d),
                            bf(w_xo[l]), row(ffn2_norm[l]), bf(ffn2_w1[l]), bf(ffn2_w3[l]),
                            bf(ffn2_w2[l]), row(final_norm), ts=ts, ff_chunk=ff_chunk)
    return x
```

```python
import functools

import jax
import jax.numpy as jnp
from jax import lax
from jax.experimental import pallas as pl
from jax.experimental.pallas import tpu as pltpu

EPS = 1e-6
RET_HEADS = 4
RET_DK = 128
RET_DV = 256
RET_CHUNK = 128
ROPE_BASE = 10000.0
LRU_BLOCKS = 8
CONV_W = 4
LRU_C = 8.0
X_HEADS = 4

SUBLANES = 8
MXU_COLS = 256
VMEM_LIMIT_BYTES = 56 * 1024 * 1024

BF16 = jnp.bfloat16
F32 = jnp.float32


def _dot(a, b):
    return jnp.dot(a, b, preferred_element_type=F32)


def _dot_slabs(a, w_ref, lo=0, hi=None, rows=slice(None)):
    hi = w_ref.shape[0] if hi is None else hi
    parts = [_dot(a, w_ref[c, rows, :]) for c in range(lo, hi)]
    return parts[0] if len(parts) == 1 else jnp.concatenate(parts, axis=1)


def _dot_cols(a, b):
    parts = [_dot(a, b[:, c:c + MXU_COLS]) for c in range(0, b.shape[1], MXU_COLS)]
    return parts[0] if len(parts) == 1 else jnp.concatenate(parts, axis=1)


def _rms(x, g):
    ms = jnp.mean(x * x, axis=-1, keepdims=True)
    return x * lax.rsqrt(ms + EPS) * g


def _const_spec(shape):
    nd = len(shape)
    return pl.BlockSpec(shape, lambda *_: (0,) * nd, pipeline_mode=pl.Buffered(1))


def _swiglu_half_step(x, g, w1_ref, w3_ref, w2_ref):
    h = _rms(x, g).astype(BF16)
    acc = None
    for j in range(w1_ref.shape[0]):
        a = _dot(h, w1_ref[j])
        b = _dot(h, w3_ref[j])
        hid = (a * jax.nn.sigmoid(a) * b).astype(BF16)
        part = _dot_slabs(hid, w2_ref, rows=slice(j * MXU_COLS, (j + 1) * MXU_COLS))
        acc = part if acc is None else acc + part
    return x + 0.5 * acc


def _ffn_kernel(x_ref, g_ref, w1_ref, w3_ref, w2_ref, o_ref):
    o_ref[...] = _swiglu_half_step(x_ref[...], g_ref[...], w1_ref, w3_ref, w2_ref)


def _ffn_call(x2d, g, w1, w3, w2, *, tm):
    t, d = x2d.shape
    return pl.pallas_call(
        _ffn_kernel,
        out_shape=jax.ShapeDtypeStruct((t, d), F32),
        grid=(t // tm,),
        in_specs=[
            pl.BlockSpec((tm, d), lambda i: (i, 0)),
            _const_spec((1, d)),
            _const_spec(w1.shape),
            _const_spec(w3.shape),
            _const_spec(w2.shape),
        ],
        out_specs=pl.BlockSpec((tm, d), lambda i: (i, 0)),
        compiler_params=pltpu.CompilerParams(
            dimension_semantics=("arbitrary",), vmem_limit_bytes=VMEM_LIMIT_BYTES),
        name="ffn1",
    )(x2d, g, w1, w3, w2)


def _mixer_kernel(x_ref, g_ref, w_in_ref, w_bg_ref, b_bg_ref,
                  cosq_ref, sinq_ref, dmat_ref, qdec_ref, kdec_ref, cdec_ref,
                  gn_ref, w_ro_ref,
                  cw_ref, cb_ref, w_ri_ref, b_r_ref, b_i_ref, lam_ref, w_lo_ref,
                  w_out_ref, perm_ref, permt_ref, o_ref,
                  state_ref, hcar_ref, prev_ref, cbuf_ref, hloc_ref, pcum_ref, *, ts):
    d = x_ref.shape[-1]
    qw = RET_HEADS * RET_DK
    vw = RET_HEADS * RET_DV
    lw = lam_ref.shape[-1]
    lb = lw // LRU_BLOCKS
    widths = (qw, qw, vw, vw, lw, lw)
    o_q, o_k, o_v, o_g, o_x, o_gl, o_end = [sum(widths[:i]) // MXU_COLS for i in range(7)]

    @pl.when(pl.program_id(1) == 0)
    def _():
        state_ref[...] = jnp.zeros_like(state_ref)
        hcar_ref[...] = jnp.zeros_like(hcar_ref)
        prev_ref[...] = jnp.zeros_like(prev_ref)

    x = x_ref[0]
    h = _rms(x, g_ref[...]).astype(BF16)

    q = _dot_slabs(h, w_in_ref, o_q, o_k)
    k = _dot_slabs(h, w_in_ref, o_k, o_v)
    v = _dot_slabs(h, w_in_ref, o_v, o_g).astype(BF16)
    g_ret = _dot_slabs(h, w_in_ref, o_g, o_x)
    cosq = cosq_ref[...]
    sinq = sinq_ref[...]
    k_scale = RET_DK ** -0.5
    gated = []
    for hd in range(RET_HEADS):
        qs = slice(hd * RET_DK, (hd + 1) * RET_DK)
        vs = slice(hd * RET_DV, (hd + 1) * RET_DV)
        qh = q[:, qs]
        kh = k[:, qs]
        qh = qh * cosq + pltpu.roll(qh, RET_DK // 2, axis=1) * sinq
        kh = (kh * cosq + pltpu.roll(kh, RET_DK // 2, axis=1) * sinq) * k_scale
        dmat = dmat_ref[hd]
        qdec = qdec_ref[hd]
        kdec = kdec_ref[hd]
        cdec = cdec_ref[hd]
        gated_h = []
        for c in range(ts // RET_CHUNK):
            rows = slice(c * RET_CHUNK, (c + 1) * RET_CHUNK)
            qc = qh[rows]
            kc = kh[rows]
            vc = v[rows, vs]
            st = state_ref[hd]
            scores = lax.dot_general(qc.astype(BF16), kc.astype(BF16),
                                     (((1,), (1,)), ((), ())),
                                     preferred_element_type=F32) * dmat
            inner = _dot(scores.astype(BF16), vc)
            cross = _dot((qc * qdec).astype(BF16), st.astype(BF16))
            kv = lax.dot_general((kc * kdec).astype(BF16), vc,
                                 (((0,), (0,)), ((), ())),
                                 preferred_element_type=F32)
            state_ref[hd] = cdec * st + kv
            y = inner + cross
            mu = jnp.mean(y, axis=-1, keepdims=True)
            yc = y - mu
            var = jnp.mean(yc * yc, axis=-1, keepdims=True)
            yn = yc * lax.rsqrt(var + EPS) * gn_ref[:, vs]
            gt = g_ret[rows, vs]
            gated_h.append((gt * jax.nn.sigmoid(gt) * yn).astype(BF16))
        gated.append(jnp.concatenate(gated_h, axis=0))
    y_ret = _dot_slabs(jnp.concatenate(gated, axis=1), w_ro_ref)

    seg = ts // SUBLANES
    hist = CONV_W - 1
    hp = _dot_cols(perm_ref[...], h).astype(BF16)
    x_lru = _dot_slabs(hp, w_in_ref, o_x, o_gl)
    sub = lax.broadcasted_iota(jnp.int32, (SUBLANES, lw), 0)
    for i in range(hist):
        cur = x_lru[(seg - hist + i) * SUBLANES:(seg - hist + i + 1) * SUBLANES, :]
        prv = prev_ref[i * SUBLANES:(i + 1) * SUBLANES, :]
        cbuf_ref[i * SUBLANES:(i + 1) * SUBLANES, :] = jnp.where(
            sub == 0, pltpu.roll(prv, 1, axis=0), pltpu.roll(cur, 1, axis=0))
    cbuf_ref[hist * SUBLANES:, :] = x_lru
    prev_ref[...] = x_lru[(seg - hist) * SUBLANES:, :]
    xc = x_lru * cw_ref[CONV_W - 1:CONV_W, :] + cb_ref[...]
    for tap in range(CONV_W - 1):
        xc = xc + cbuf_ref[tap * SUBLANES:tap * SUBLANES + ts, :] * cw_ref[tap:tap + 1, :]

    xcb = xc.astype(BF16)
    ri = [_dot(xcb[:, gb * lb:(gb + 1) * lb], w_ri_ref[gb]) for gb in range(LRU_BLOCKS)]
    r = jax.nn.sigmoid(jnp.concatenate([p[:, :lb] for p in ri], axis=1) + b_r_ref[...])
    ig = jax.nn.sigmoid(jnp.concatenate([p[:, lb:] for p in ri], axis=1) + b_i_ref[...])
    nlam = -lam_ref[...]
    softplus = jnp.maximum(nlam, 0.0) + jnp.log1p(jnp.exp(-jnp.abs(nlam)))
    log_a = (-LRU_C * softplus) * r
    a = jnp.exp(log_a)
    th = jnp.tanh(log_a)
    mult = jnp.sqrt(-2.0 * th / (1.0 - th))
    bx = mult * (ig * xc)

    for j in range(seg):
        rows = slice(j * SUBLANES, (j + 1) * SUBLANES)
        if j == 0:
            hloc, pcum = bx[rows], a[rows]
        else:
            hloc, pcum = a[rows] * hloc + bx[rows], a[rows] * pcum
        hloc_ref[rows, :] = hloc
        pcum_ref[rows, :] = pcum
    start = jnp.broadcast_to(hcar_ref[...], (SUBLANES, lw))
    for s in range(1, SUBLANES):
        start = jnp.where(sub == s, pltpu.roll(hloc + pcum * start, 1, axis=0), start)
    hcar_ref[...] = (hloc + pcum * start)[SUBLANES - 1:SUBLANES, :]
    hl = (hloc_ref[...].reshape(seg, SUBLANES, lw)
          + pcum_ref[...].reshape(seg, SUBLANES, lw) * start[None]).reshape(ts, lw)

    g_lru = _dot_slabs(hp, w_in_ref, o_gl, o_end)
    hg = (hl * jax.nn.gelu(g_lru)).astype(BF16)
    y_lru = _dot_slabs(_dot_cols(permt_ref[...], hg).astype(BF16), w_lo_ref)

    gates = jax.nn.sigmoid(_dot_slabs(h, w_bg_ref) + b_bg_ref[...])
    merged = gates[:, :d] * y_ret + gates[:, d:] * y_lru
    o_ref[0] = x + _dot_slabs(merged.astype(BF16), w_out_ref)


def _mixer_call(x, g, w_in, w_bg, b_bg, cosq, sinq, dmat, qdec, kdec, cdec, gn, w_ro,
                cw, cb, w_ri, b_r, b_i, lam, w_lo, w_out, *, ts):
    b, s, d = x.shape
    lw = lam.shape[-1]
    consts = (g, w_in, w_bg, b_bg)
    perm = _segment_permutation(ts)
    consts2 = (dmat, qdec, kdec, cdec, gn, w_ro, cw, cb, w_ri, b_r, b_i, lam, w_lo, w_out,
               perm, perm.T)
    rope_spec = pl.BlockSpec((ts, RET_DK), lambda bi, si: (si, 0))
    return pl.pallas_call(
        functools.partial(_mixer_kernel, ts=ts),
        out_shape=jax.ShapeDtypeStruct((b, s, d), F32),
        grid=(b, s // ts),
        in_specs=([pl.BlockSpec((1, ts, d), lambda bi, si: (bi, si, 0))]
                  + [_const_spec(c.shape) for c in consts]
                  + [rope_spec, rope_spec]
                  + [_const_spec(c.shape) for c in consts2]),
        out_specs=pl.BlockSpec((1, ts, d), lambda bi, si: (bi, si, 0)),
        scratch_shapes=[
            pltpu.VMEM((RET_HEADS, RET_DK, RET_DV), F32),
            pltpu.VMEM((1, lw), F32),
            pltpu.VMEM(((CONV_W - 1) * SUBLANES, lw), F32),
            pltpu.VMEM((ts + (CONV_W - 1) * SUBLANES, lw), F32),
            pltpu.VMEM((ts, lw), F32),
            pltpu.VMEM((ts, lw), F32),
        ],
        compiler_params=pltpu.CompilerParams(
            dimension_semantics=("arbitrary", "arbitrary"), vmem_limit_bytes=VMEM_LIMIT_BYTES),
        name="mixer",
    )(x, *consts, cosq, sinq, *consts2)


def _memkv_kernel(m_ref, g_ref, wk_ref, wv_ref, k_ref, v_ref):
    m = _rms(m_ref[0], g_ref[...]).astype(BF16)
    for hh in range(X_HEADS):
        k_ref[0, hh] = _dot(m, wk_ref[hh]).astype(BF16)
        v_ref[0, hh] = _dot(m, wv_ref[hh]).astype(BF16)


def _memkv_call(mem, g, wk, wv):
    b, m, d = mem.shape
    kv_shape = jax.ShapeDtypeStruct((b, X_HEADS, m, d // X_HEADS), BF16)
    kv_spec = pl.BlockSpec((1, X_HEADS, m, d // X_HEADS), lambda i: (i, 0, 0, 0))
    return pl.pallas_call(
        _memkv_kernel,
        out_shape=(kv_shape, kv_shape),
        grid=(b,),
        in_specs=[pl.BlockSpec((1, m, d), lambda i: (i, 0, 0)), _const_spec((1, d)),
                  _const_spec(wk.shape), _const_spec(wv.shape)],
        out_specs=(kv_spec, kv_spec),
        compiler_params=pltpu.CompilerParams(
            dimension_semantics=("arbitrary",), vmem_limit_bytes=VMEM_LIMIT_BYTES),
        name="mem_kv",
    )(mem, g, wk, wv)


def _xattn_ffn_kernel(x_ref, gq_ref, wq_ref, k_ref, v_ref, wo_ref,
                      g2_ref, w1_ref, w3_ref, w2_ref, gf_ref, o_ref):
    x = x_ref[0]
    d = x.shape[-1]
    hd = d // X_HEADS
    hq = _rms(x, gq_ref[...]).astype(BF16)
    xq = [(_dot(hq, wq_ref[hh]) * (hd ** -0.5)).astype(BF16) for hh in range(X_HEADS)]
    xo = []
    for hh in range(X_HEADS):
        sc = lax.dot_general(xq[hh], k_ref[0, hh], (((1,), (1,)), ((), ())),
                             preferred_element_type=F32)
        e = jnp.exp(sc - jnp.max(sc, axis=-1, keepdims=True))
        pv = _dot(e.astype(BF16), v_ref[0, hh])
        xo.append((pv / jnp.sum(e, axis=-1, keepdims=True)).astype(BF16))
    x3 = x + _dot_slabs(jnp.concatenate(xo, axis=1), wo_ref)
    y = _swiglu_half_step(x3, g2_ref[...], w1_ref, w3_ref, w2_ref)
    o_ref[0] = _rms(y, gf_ref[...])


def _xattn_ffn_call(x, gq, wq, xk, xv, wo, g2, w1, w3, w2, gf, *, ts):
    b, s, d = x.shape
    kv_spec = pl.BlockSpec((1,) + xk.shape[1:], lambda bi, si: (bi, 0, 0, 0))
    return pl.pallas_call(
        _xattn_ffn_kernel,
        out_shape=jax.ShapeDtypeStruct((b, s, d), F32),
        grid=(b, s // ts),
        in_specs=[
            pl.BlockSpec((1, ts, d), lambda bi, si: (bi, si, 0)),
            _const_spec((1, d)),
            _const_spec(wq.shape),
            kv_spec,
            kv_spec,
            _const_spec(wo.shape),
            _const_spec((1, d)),
            _const_spec(w1.shape),
            _const_spec(w3.shape),
            _const_spec(w2.shape),
            _const_spec((1, d)),
        ],
        out_specs=pl.BlockSpec((1, ts, d), lambda bi, si: (bi, si, 0)),
        compiler_params=pltpu.CompilerParams(
            dimension_semantics=("arbitrary", "arbitrary"), vmem_limit_bytes=VMEM_LIMIT_BYTES),
        name="xattn_ffn2",
    )(x, gq, wq, xk, xv, wo, g2, w1, w3, w2, gf)


def _slabs(w):
    kdim, n = w.shape
    return w.astype(BF16).reshape(kdim, n // MXU_COLS, MXU_COLS).transpose(1, 0, 2)


def _rope_tables(s):
    pos = jnp.arange(s, dtype=F32)
    inv_freq = ROPE_BASE ** (-jnp.arange(0, RET_DK, 2, dtype=F32) / RET_DK)
    ang = pos[:, None] * inv_freq[None, :]
    cos, sin = jnp.cos(ang), jnp.sin(ang)
    return jnp.concatenate([cos, cos], axis=-1), jnp.concatenate([-sin, sin], axis=-1)


def _segment_permutation(ts):
    seg = ts // SUBLANES
    r = jnp.arange(ts)
    src = (r % SUBLANES) * seg + r // SUBLANES
    return (src[:, None] == jnp.arange(ts)[None, :]).astype(BF16)


def _decay_tables():
    log_gamma = jnp.log(1.0 - 2.0 ** (-5.0 - jnp.arange(RET_HEADS, dtype=F32)))
    lg = log_gamma[:, None, None]
    pos = jnp.arange(RET_CHUNK, dtype=F32)
    rel = pos[:, None] - pos[None, :]
    dmat = jnp.where(rel[None] >= 0, jnp.exp(rel[None] * lg), 0.0)
    ones = jnp.ones((1, RET_CHUNK, RET_DK), F32)
    qdec = jnp.exp((pos + 1.0)[None, :, None] * lg) * ones
    kdec = jnp.exp((RET_CHUNK - 1.0 - pos)[None, :, None] * lg) * ones
    cdec = jnp.exp(RET_CHUNK * lg) * jnp.ones((1, RET_DK, RET_DV), F32)
    return dmat, qdec, kdec, cdec


def kernel(x, mem, ffn1_norm, ffn1_w1, ffn1_w3, ffn1_w2, mix_norm, w_in, ret_gn, w_ret_o, conv_w, conv_b, w_rgate, b_rgate, w_igate, b_igate, lru_lambda, w_lru_o, w_branch_gate, b_branch_gate, w_out, xattn_norm, mem_norm, w_xq, w_xk, w_xv, w_xo, ffn2_norm, ffn2_w1, ffn2_w3, ffn2_w2, final_norm):
    b, s, d = x.shape
    depth = ffn1_norm.shape[0]
    tm, ts, ts_x = 512, 256, 512
    bf = _slabs
    row = lambda p: p.reshape(1, -1)
    cosq, sinq = _rope_tables(s)
    dmat, qdec, kdec, cdec = _decay_tables()
    for l in range(depth):
        x = _ffn_call(x.reshape(b * s, d), row(ffn1_norm[l]), bf(ffn1_w1[l]), bf(ffn1_w3[l]),
                      bf(ffn1_w2[l]), tm=tm).reshape(b, s, d)
        w_ri = jnp.concatenate([w_rgate[l], w_igate[l]], axis=-1).astype(BF16)
        x = _mixer_call(x, row(mix_norm[l]), bf(w_in[l]), bf(w_branch_gate[l]), row(b_branch_gate[l]),
                        cosq, sinq, dmat, qdec, kdec, cdec, row(ret_gn[l]), bf(w_ret_o[l]),
                        conv_w[l], row(conv_b[l]), w_ri, row(b_rgate[l]), row(b_igate[l]),
                        row(lru_lambda[l]), bf(w_lru_o[l]), bf(w_out[l]), ts=ts)
        xk, xv = _memkv_call(mem, row(mem_norm[l]), bf(w_xk[l]), bf(w_xv[l]))
        assert depth == 1
        x = _xattn_ffn_call(x, row(xattn_norm[l]), bf(w_xq[l]), xk, xv,
                            bf(w_xo[l]), row(ffn2_norm[l]), bf(ffn2_w1[l]), bf(ffn2_w3[l]),
                            bf(ffn2_w2[l]), row(final_norm), ts=ts_x)
    return x
```

```python
import functools

import jax
import jax.numpy as jnp
from jax import lax
from jax.experimental import pallas as pl
from jax.experimental.pallas import tpu as pltpu

EPS = 1e-6
RET_HEADS = 4
RET_DK = 128
RET_DV = 256
RET_CHUNK = 128
ROPE_BASE = 10000.0
LRU_BLOCKS = 8
CONV_W = 4
LRU_C = 8.0
X_HEADS = 4

SUBLANES = 8
BF16_TILE_ROWS = 16
MXU_COLS = 256
VMEM_LIMIT_BYTES = 56 * 1024 * 1024

BF16 = jnp.bfloat16
F32 = jnp.float32


def _dot(a, b):
    return jnp.dot(a, b, preferred_element_type=F32)


def _dot_slabs(a, w_ref, lo=0, hi=None, rows=slice(None)):
    hi = w_ref.shape[0] if hi is None else hi
    parts = [_dot(a, w_ref[c, rows, :]) for c in range(lo, hi)]
    return parts[0] if len(parts) == 1 else jnp.concatenate(parts, axis=1)


def _dot_cols(a, b):
    parts = [_dot(a, b[:, c:c + MXU_COLS]) for c in range(0, b.shape[1], MXU_COLS)]
    return parts[0] if len(parts) == 1 else jnp.concatenate(parts, axis=1)


def _rms(x, g):
    ms = jnp.mean(x * x, axis=-1, keepdims=True)
    return x * lax.rsqrt(ms + EPS) * g


def _const_spec(shape):
    nd = len(shape)
    return pl.BlockSpec(shape, lambda *_: (0,) * nd, pipeline_mode=pl.Buffered(1))


def _swiglu_half_step(x, g, w1_ref, w3_ref, w2_ref):
    h = _rms(x, g).astype(BF16)
    acc = None
    for j in range(w1_ref.shape[0]):
        a = _dot(h, w1_ref[j])
        b = _dot(h, w3_ref[j])
        hid = (a * jax.nn.sigmoid(a) * b).astype(BF16)
        part = _dot_slabs(hid, w2_ref, rows=slice(j * MXU_COLS, (j + 1) * MXU_COLS))
        acc = part if acc is None else acc + part
    return x + 0.5 * acc


def _rider_blocks(w, n_steps):
    kdim = w.shape[0]
    return max(n for n in range(1, n_steps + 1)
               if kdim % n == 0 and (kdim // n) % BF16_TILE_ROWS == 0)


def _rider_specs(weights, grid):
    n_steps = 1
    for g in grid:
        n_steps *= g
    in_specs, out_specs, out_shapes, counts = [], [], [], []
    for w in weights:
        kdim, n = w.shape
        nb = _rider_blocks(w, n_steps)

        def block(*ids, nb=nb):
            lin = ids[0]
            for i, g in zip(ids[1:], grid[1:]):
                lin = lin * g + i
            return jnp.minimum(lin, nb - 1)

        in_specs.append(pl.BlockSpec((kdim // nb, n), lambda *ids, block=block: (block(*ids), 0)))
        out_specs.append(pl.BlockSpec((n // MXU_COLS, kdim // nb, MXU_COLS),
                                      lambda *ids, block=block: (0, block(*ids), 0)))
        out_shapes.append(jax.ShapeDtypeStruct((n // MXU_COLS, kdim, MXU_COLS), BF16))
        counts.append(nb)
    return in_specs, out_specs, out_shapes, counts


def _with_riders(body, n_in, n_out, counts, grid):
    n_r = len(counts)

    def kern(*refs):
        ins, r_ins = refs[:n_in], refs[n_in:n_in + n_r]
        outs = refs[n_in + n_r:n_in + n_r + n_out]
        r_outs = refs[n_in + n_r + n_out:n_in + 2 * n_r + n_out]
        scratch = refs[n_in + 2 * n_r + n_out:]
        step = pl.program_id(0)
        for axis in range(1, len(grid)):
            step = step * grid[axis] + pl.program_id(axis)
        for w_ref, o_ref, nb in zip(r_ins, r_outs, counts):
            @pl.when(step < nb)
            def _():
                for c in range(o_ref.shape[0]):
                    o_ref[c] = w_ref[:, c * MXU_COLS:(c + 1) * MXU_COLS].astype(BF16)
        body(*ins, *outs, *scratch)

    return kern


def _ffn_kernel(x_ref, g_ref, w1_ref, w3_ref, w2_ref, o_ref):
    o_ref[...] = _swiglu_half_step(x_ref[...], g_ref[...], w1_ref, w3_ref, w2_ref)


def _ffn_call(x2d, g, w1, w3, w2, riders, *, tm):
    t, d = x2d.shape
    grid = (t // tm,)
    r_in, r_out, r_shapes, counts = _rider_specs(riders, grid)
    return pl.pallas_call(
        _with_riders(_ffn_kernel, 5, 1, counts, grid),
        out_shape=[jax.ShapeDtypeStruct((t, d), F32)] + r_shapes,
        grid=grid,
        in_specs=[
            pl.BlockSpec((tm, d), lambda i: (i, 0)),
            _const_spec((1, d)),
            _const_spec(w1.shape),
            _const_spec(w3.shape),
            _const_spec(w2.shape),
        ] + r_in,
        out_specs=[pl.BlockSpec((tm, d), lambda i: (i, 0))] + r_out,
        compiler_params=pltpu.CompilerParams(
            dimension_semantics=("arbitrary",), vmem_limit_bytes=VMEM_LIMIT_BYTES),
        name="ffn1",
    )(x2d, g, w1, w3, w2, *riders)


def _mixer_kernel(x_ref, g_ref, w_in_ref, w_bg_ref, b_bg_ref,
                  cosq_ref, sinq_ref, dmat_ref, qdec_ref, kdec_ref, cdec_ref,
                  gn_ref, w_ro_ref,
                  cw_ref, cb_ref, w_ri_ref, b_r_ref, b_i_ref, lam_ref, w_lo_ref,
                  w_out_ref, perm_ref, permt_ref, o_ref,
                  state_ref, hcar_ref, prev_ref, cbuf_ref, hloc_ref, pcum_ref, *, ts):
    d = x_ref.shape[-1]
    qw = RET_HEADS * RET_DK
    vw = RET_HEADS * RET_DV
    lw = lam_ref.shape[-1]
    lb = lw // LRU_BLOCKS
    widths = (qw, qw, vw, vw, lw, lw)
    o_q, o_k, o_v, o_g, o_x, o_gl, o_end = [sum(widths[:i]) // MXU_COLS for i in range(7)]

    @pl.when(pl.program_id(1) == 0)
    def _():
        state_ref[...] = jnp.zeros_like(state_ref)
        hcar_ref[...] = jnp.zeros_like(hcar_ref)
        prev_ref[...] = jnp.zeros_like(prev_ref)

    x = x_ref[0]
    h = _rms(x, g_ref[...]).astype(BF16)

    q = _dot_slabs(h, w_in_ref, o_q, o_k)
    k = _dot_slabs(h, w_in_ref, o_k, o_v)
    v = _dot_slabs(h, w_in_ref, o_v, o_g).astype(BF16)
    g_ret = _dot_slabs(h, w_in_ref, o_g, o_x)
    cosq = cosq_ref[...]
    sinq = sinq_ref[...]
    k_scale = RET_DK ** -0.5
    gated = []
    for hd in range(RET_HEADS):
        qs = slice(hd * RET_DK, (hd + 1) * RET_DK)
        vs = slice(hd * RET_DV, (hd + 1) * RET_DV)
        qh = q[:, qs]
        kh = k[:, qs]
        qh = qh * cosq + pltpu.roll(qh, RET_DK // 2, axis=1) * sinq
        kh = (kh * cosq + pltpu.roll(kh, RET_DK // 2, axis=1) * sinq) * k_scale
        dmat = dmat_ref[hd]
        qdec = qdec_ref[hd]
        kdec = kdec_ref[hd]
        cdec = cdec_ref[hd]
        gated_h = []
        for c in range(ts // RET_CHUNK):
            rows = slice(c * RET_CHUNK, (c + 1) * RET_CHUNK)
            qc = qh[rows]
            kc = kh[rows]
            vc = v[rows, vs]
            st = state_ref[hd]
            scores = lax.dot_general(qc.astype(BF16), kc.astype(BF16),
                                     (((1,), (1,)), ((), ())),
                                     preferred_element_type=F32) * dmat
            inner = _dot(scores.astype(BF16), vc)
            cross = _dot((qc * qdec).astype(BF16), st.astype(BF16))
            kv = lax.dot_general((kc * kdec).astype(BF16), vc,
                                 (((0,), (0,)), ((), ())),
                                 preferred_element_type=F32)
            state_ref[hd] = cdec * st + kv
            y = inner + cross
            mu = jnp.mean(y, axis=-1, keepdims=True)
            yc = y - mu
            var = jnp.mean(yc * yc, axis=-1, keepdims=True)
            yn = yc * lax.rsqrt(var + EPS) * gn_ref[:, vs]
            gt = g_ret[rows, vs]
            gated_h.append((gt * jax.nn.sigmoid(gt) * yn).astype(BF16))
        gated.append(jnp.concatenate(gated_h, axis=0))
    y_ret = _dot_slabs(jnp.concatenate(gated, axis=1), w_ro_ref)

    seg = ts // SUBLANES
    hist = CONV_W - 1
    hp = _dot_cols(perm_ref[...], h).astype(BF16)
    x_lru = _dot_slabs(hp, w_in_ref, o_x, o_gl)
    sub = lax.broadcasted_iota(jnp.int32, (SUBLANES, lw), 0)
    for i in range(hist):
        cur = x_lru[(seg - hist + i) * SUBLANES:(seg - hist + i + 1) * SUBLANES, :]
        prv = prev_ref[i * SUBLANES:(i + 1) * SUBLANES, :]
        cbuf_ref[i * SUBLANES:(i + 1) * SUBLANES, :] = jnp.where(
            sub == 0, pltpu.roll(prv, 1, axis=0), pltpu.roll(cur, 1, axis=0))
    cbuf_ref[hist * SUBLANES:, :] = x_lru
    prev_ref[...] = x_lru[(seg - hist) * SUBLANES:, :]
    xc = x_lru * cw_ref[CONV_W - 1:CONV_W, :] + cb_ref[...]
    for tap in range(CONV_W - 1):
        xc = xc + cbuf_ref[tap * SUBLANES:tap * SUBLANES + ts, :] * cw_ref[tap:tap + 1, :]

    xcb = xc.astype(BF16)
    ri = [_dot(xcb[:, gb * lb:(gb + 1) * lb], w_ri_ref[gb]) for gb in range(LRU_BLOCKS)]
    r = jax.nn.sigmoid(jnp.concatenate([p[:, :lb] for p in ri], axis=1) + b_r_ref[...])
    ig = jax.nn.sigmoid(jnp.concatenate([p[:, lb:] for p in ri], axis=1) + b_i_ref[...])
    nlam = -lam_ref[...]
    softplus = jnp.maximum(nlam, 0.0) + jnp.log1p(jnp.exp(-jnp.abs(nlam)))
    log_a = (-LRU_C * softplus) * r
    a = jnp.exp(log_a)
    th = jnp.tanh(log_a)
    mult = jnp.sqrt(-2.0 * th / (1.0 - th))
    bx = mult * (ig * xc)

    for j in range(seg):
        rows = slice(j * SUBLANES, (j + 1) * SUBLANES)
        if j == 0:
            hloc, pcum = bx[rows], a[rows]
        else:
            hloc, pcum = a[rows] * hloc + bx[rows], a[rows] * pcum
        hloc_ref[rows, :] = hloc
        pcum_ref[rows, :] = pcum
    start = jnp.broadcast_to(hcar_ref[...], (SUBLANES, lw))
    for s in range(1, SUBLANES):
        start = jnp.where(sub == s, pltpu.roll(hloc + pcum * start, 1, axis=0), start)
    hcar_ref[...] = (hloc + pcum * start)[SUBLANES - 1:SUBLANES, :]
    hl = (hloc_ref[...].reshape(seg, SUBLANES, lw)
          + pcum_ref[...].reshape(seg, SUBLANES, lw) * start[None]).reshape(ts, lw)

    g_lru = _dot_slabs(hp, w_in_ref, o_gl, o_end)
    hg = (hl * jax.nn.gelu(g_lru)).astype(BF16)
    y_lru = _dot_slabs(_dot_cols(permt_ref[...], hg).astype(BF16), w_lo_ref)

    gates = jax.nn.sigmoid(_dot_slabs(h, w_bg_ref) + b_bg_ref[...])
    merged = gates[:, :d] * y_ret + gates[:, d:] * y_lru
    o_ref[0] = x + _dot_slabs(merged.astype(BF16), w_out_ref)


def _mixer_call(x, g, w_in, w_bg, b_bg, cosq, sinq, dmat, qdec, kdec, cdec, gn, w_ro,
                cw, cb, w_ri, b_r, b_i, lam, w_lo, w_out, riders, *, ts):
    b, s, d = x.shape
    grid = (b, s // ts)
    r_in, r_out, r_shapes, counts = _rider_specs(riders, grid)
    lw = lam.shape[-1]
    consts = (g, w_in, w_bg, b_bg)
    perm = _segment_permutation(ts)
    consts2 = (dmat, qdec, kdec, cdec, gn, w_ro, cw, cb, w_ri, b_r, b_i, lam, w_lo, w_out,
               perm, perm.T)
    rope_spec = pl.BlockSpec((ts, RET_DK), lambda bi, si: (si, 0))
    n_in = 3 + len(consts) + len(consts2)
    return pl.pallas_call(
        _with_riders(functools.partial(_mixer_kernel, ts=ts), n_in, 1, counts, grid),
        out_shape=[jax.ShapeDtypeStruct((b, s, d), F32)] + r_shapes,
        grid=grid,
        in_specs=([pl.BlockSpec((1, ts, d), lambda bi, si: (bi, si, 0))]
                  + [_const_spec(c.shape) for c in consts]
                  + [rope_spec, rope_spec]
                  + [_const_spec(c.shape) for c in consts2] + r_in),
        out_specs=[pl.BlockSpec((1, ts, d), lambda bi, si: (bi, si, 0))] + r_out,
        scratch_shapes=[
            pltpu.VMEM((RET_HEADS, RET_DK, RET_DV), F32),
            pltpu.VMEM((1, lw), F32),
            pltpu.VMEM(((CONV_W - 1) * SUBLANES, lw), F32),
            pltpu.VMEM((ts + (CONV_W - 1) * SUBLANES, lw), F32),
            pltpu.VMEM((ts, lw), F32),
            pltpu.VMEM((ts, lw), F32),
        ],
        compiler_params=pltpu.CompilerParams(
            dimension_semantics=("arbitrary", "arbitrary"), vmem_limit_bytes=VMEM_LIMIT_BYTES),
        name="mixer",
    )(x, *consts, cosq, sinq, *consts2, *riders)


def _memkv_kernel(m_ref, g_ref, wk_ref, wv_ref, k_ref, v_ref, wkb_ref, wvb_ref):
    hd = k_ref.shape[-1]

    @pl.when(pl.program_id(0) == 0)
    def _():
        for hh in range(X_HEADS):
            wkb_ref[hh] = wk_ref[:, hh * hd:(hh + 1) * hd].astype(BF16)
            wvb_ref[hh] = wv_ref[:, hh * hd:(hh + 1) * hd].astype(BF16)

    m = _rms(m_ref[0], g_ref[...]).astype(BF16)
    for hh in range(X_HEADS):
        k_ref[0, hh] = _dot(m, wkb_ref[hh]).astype(BF16)
        v_ref[0, hh] = _dot(m, wvb_ref[hh]).astype(BF16)


def _memkv_call(mem, g, wk, wv, riders):
    b, m, d = mem.shape
    hd = d // X_HEADS
    grid = (b,)
    r_in, r_out, r_shapes, counts = _rider_specs(riders, grid)
    kv_shape = jax.ShapeDtypeStruct((b, X_HEADS, m, hd), BF16)
    kv_spec = pl.BlockSpec((1, X_HEADS, m, hd), lambda i: (i, 0, 0, 0))
    return pl.pallas_call(
        _with_riders(_memkv_kernel, 4, 2, counts, grid),
        out_shape=[kv_shape, kv_shape] + r_shapes,
        grid=grid,
        in_specs=[pl.BlockSpec((1, m, d), lambda i: (i, 0, 0)), _const_spec((1, d)),
                  _const_spec(wk.shape), _const_spec(wv.shape)] + r_in,
        out_specs=[kv_spec, kv_spec] + r_out,
        scratch_shapes=[pltpu.VMEM((X_HEADS, d, hd), BF16), pltpu.VMEM((X_HEADS, d, hd), BF16)],
        compiler_params=pltpu.CompilerParams(
            dimension_semantics=("arbitrary",), vmem_limit_bytes=VMEM_LIMIT_BYTES),
        name="mem_kv",
    )(mem, g, wk, wv, *riders)


def _xattn_ffn_kernel(x_ref, gq_ref, wq_ref, k_ref, v_ref, wo_ref,
                      g2_ref, w1_ref, w3_ref, w2_ref, gf_ref, o_ref):
    x = x_ref[0]
    d = x.shape[-1]
    hd = d // X_HEADS
    hq = _rms(x, gq_ref[...]).astype(BF16)
    xq = [(_dot(hq, wq_ref[hh]) * (hd ** -0.5)).astype(BF16) for hh in range(X_HEADS)]
    xo = []
    for hh in range(X_HEADS):
        sc = lax.dot_general(xq[hh], k_ref[0, hh], (((1,), (1,)), ((), ())),
                             preferred_element_type=F32)
        e = jnp.exp(sc - jnp.max(sc, axis=-1, keepdims=True))
        pv = _dot(e.astype(BF16), v_ref[0, hh])
        xo.append((pv / jnp.sum(e, axis=-1, keepdims=True)).astype(BF16))
    x3 = x + _dot_slabs(jnp.concatenate(xo, axis=1), wo_ref)
    y = _swiglu_half_step(x3, g2_ref[...], w1_ref, w3_ref, w2_ref)
    o_ref[0] = _rms(y, gf_ref[...])


def _xattn_ffn_call(x, gq, wq, xk, xv, wo, g2, w1, w3, w2, gf, *, ts):
    b, s, d = x.shape
    kv_spec = pl.BlockSpec((1,) + xk.shape[1:], lambda bi, si: (bi, 0, 0, 0))
    return pl.pallas_call(
        _xattn_ffn_kernel,
        out_shape=jax.ShapeDtypeStruct((b, s, d), F32),
        grid=(b, s // ts),
        in_specs=[
            pl.BlockSpec((1, ts, d), lambda bi, si: (bi, si, 0)),
            _const_spec((1, d)),
            _const_spec(wq.shape),
            kv_spec,
            kv_spec,
            _const_spec(wo.shape),
            _const_spec((1, d)),
            _const_spec(w1.shape),
            _const_spec(w3.shape),
            _const_spec(w2.shape),
            _const_spec((1, d)),
        ],
        out_specs=pl.BlockSpec((1, ts, d), lambda bi, si: (bi, si, 0)),
        compiler_params=pltpu.CompilerParams(
            dimension_semantics=("arbitrary", "arbitrary"), vmem_limit_bytes=VMEM_LIMIT_BYTES),
        name="xattn_ffn2",
    )(x, gq, wq, xk, xv, wo, g2, w1, w3, w2, gf)


def _rope_tables(s):
    pos = jnp.arange(s, dtype=F32)
    inv_freq = ROPE_BASE ** (-jnp.arange(0, RET_DK, 2, dtype=F32) / RET_DK)
    ang = pos[:, None] * inv_freq[None, :]
    cos, sin = jnp.cos(ang), jnp.sin(ang)
    return jnp.concatenate([cos, cos], axis=-1), jnp.concatenate([-sin, sin], axis=-1)


def _segment_permutation(ts):
    seg = ts // SUBLANES
    r = jnp.arange(ts)
    src = (r % SUBLANES) * seg + r // SUBLANES
    return (src[:, None] == jnp.arange(ts)[None, :]).astype(BF16)


def _decay_tables():
    log_gamma = jnp.log(1.0 - 2.0 ** (-5.0 - jnp.arange(RET_HEADS, dtype=F32)))
    lg = log_gamma[:, None, None]
    pos = jnp.arange(RET_CHUNK, dtype=F32)
    rel = pos[:, None] - pos[None, :]
    dmat = jnp.where(rel[None] >= 0, jnp.exp(rel[None] * lg), 0.0)
    ones = jnp.ones((1, RET_CHUNK, RET_DK), F32)
    qdec = jnp.exp((pos + 1.0)[None, :, None] * lg) * ones
    kdec = jnp.exp((RET_CHUNK - 1.0 - pos)[None, :, None] * lg) * ones
    cdec = jnp.exp(RET_CHUNK * lg) * jnp.ones((1, RET_DK, RET_DV), F32)
    return dmat, qdec, kdec, cdec


def kernel(x, mem, ffn1_norm, ffn1_w1, ffn1_w3, ffn1_w2, mix_norm, w_in, ret_gn, w_ret_o, conv_w, conv_b, w_rgate, b_rgate, w_igate, b_igate, lru_lambda, w_lru_o, w_branch_gate, b_branch_gate, w_out, xattn_norm, mem_norm, w_xq, w_xk, w_xv, w_xo, ffn2_norm, ffn2_w1, ffn2_w3, ffn2_w2, final_norm):
    b, s, d = x.shape
    depth = ffn1_norm.shape[0]
    tm, ts, ts_x = 512, 256, 512
    row = lambda p: p.reshape(1, -1)
    cosq, sinq = _rope_tables(s)
    dmat, qdec, kdec, cdec = _decay_tables()
    assert depth == 1
    l = 0
    xk, xv, f1_w1, f1_w3, f1_w2 = _memkv_call(
        mem, row(mem_norm[l]), w_xk[l], w_xv[l], [ffn1_w1[l], ffn1_w3[l], ffn1_w2[l]])
    x, m_in, m_bg, m_ro, m_lo, m_out = _ffn_call(
        x.reshape(b * s, d), row(ffn1_norm[l]), f1_w1, f1_w3, f1_w2,
        [w_in[l], w_branch_gate[l], w_ret_o[l], w_lru_o[l], w_out[l]], tm=tm)
    w_ri = jnp.concatenate([w_rgate[l], w_igate[l]], axis=-1).astype(BF16)
    x, x_wq, x_wo, f2_w1, f2_w3, f2_w2 = _mixer_call(
        x.reshape(b, s, d), row(mix_norm[l]), m_in, m_bg, row(b_branch_gate[l]),
        cosq, sinq, dmat, qdec, kdec, cdec, row(ret_gn[l]), m_ro,
        conv_w[l], row(conv_b[l]), w_ri, row(b_rgate[l]), row(b_igate[l]),
        row(lru_lambda[l]), m_lo, m_out,
        [w_xq[l], w_xo[l], ffn2_w1[l], ffn2_w3[l], ffn2_w2[l]], ts=ts)
    return _xattn_ffn_call(x, row(xattn_norm[l]), x_wq, xk, xv, x_wo, row(ffn2_norm[l]),
                           f2_w1, f2_w3, f2_w2, row(final_norm), ts=ts_x)
```

```python
import functools

import jax
import jax.numpy as jnp
from jax import lax
from jax.experimental import pallas as pl
from jax.experimental.pallas import tpu as pltpu

EPS = 1e-6
RET_HEADS = 4
RET_DK = 128
RET_DV = 256
RET_CHUNK = 128
ROPE_BASE = 10000.0
LRU_BLOCKS = 8
CONV_W = 4
LRU_C = 8.0
LRU_TILE = 256
X_HEADS = 4

SUBLANES = 8
BF16_TILE_ROWS = 16
MXU_COLS = 256
VMEM_LIMIT_BYTES = 56 * 1024 * 1024

BF16 = jnp.bfloat16
F32 = jnp.float32


def _dot(a, b):
    return jnp.dot(a, b, preferred_element_type=F32)


def _dot_slabs(a, w_ref, lo=0, hi=None, rows=slice(None)):
    hi = w_ref.shape[0] if hi is None else hi
    parts = [_dot(a, w_ref[c, rows, :]) for c in range(lo, hi)]
    return parts[0] if len(parts) == 1 else jnp.concatenate(parts, axis=1)


def _dot_cols(a, b):
    parts = [_dot(a, b[:, c:c + MXU_COLS]) for c in range(0, b.shape[1], MXU_COLS)]
    return parts[0] if len(parts) == 1 else jnp.concatenate(parts, axis=1)


def _rms(x, g):
    ms = jnp.mean(x * x, axis=-1, keepdims=True)
    return x * lax.rsqrt(ms + EPS) * g


def _const_spec(shape):
    nd = len(shape)
    return pl.BlockSpec(shape, lambda *_: (0,) * nd, pipeline_mode=pl.Buffered(1))


def _swiglu_half_step(x, g, w1_ref, w3_ref, w2_ref):
    h = _rms(x, g).astype(BF16)
    acc = None
    for j in range(w1_ref.shape[0]):
        a = _dot(h, w1_ref[j])
        b = _dot(h, w3_ref[j])
        hid = (a * jax.nn.sigmoid(a) * b).astype(BF16)
        part = _dot_slabs(hid, w2_ref, rows=slice(j * MXU_COLS, (j + 1) * MXU_COLS))
        acc = part if acc is None else acc + part
    return x + 0.5 * acc


def _rider_blocks(w, n_steps):
    kdim = w.shape[0]
    return max(n for n in range(1, n_steps + 1)
               if kdim % n == 0 and (kdim // n) % BF16_TILE_ROWS == 0)


def _rider_specs(weights, grid):
    n_steps = 1
    for g in grid:
        n_steps *= g
    in_specs, out_specs, out_shapes, counts = [], [], [], []
    for w in weights:
        kdim, n = w.shape
        nb = _rider_blocks(w, n_steps)

        def block(*ids, nb=nb):
            lin = ids[0]
            for i, g in zip(ids[1:], grid[1:]):
                lin = lin * g + i
            return jnp.minimum(lin, nb - 1)

        in_specs.append(pl.BlockSpec((kdim // nb, n), lambda *ids, block=block: (block(*ids), 0)))
        out_specs.append(pl.BlockSpec((n // MXU_COLS, kdim // nb, MXU_COLS),
                                      lambda *ids, block=block: (0, block(*ids), 0)))
        out_shapes.append(jax.ShapeDtypeStruct((n // MXU_COLS, kdim, MXU_COLS), BF16))
        counts.append(nb)
    return in_specs, out_specs, out_shapes, counts


def _with_riders(body, n_in, n_out, counts, grid):
    n_r = len(counts)

    def kern(*refs):
        ins, r_ins = refs[:n_in], refs[n_in:n_in + n_r]
        outs = refs[n_in + n_r:n_in + n_r + n_out]
        r_outs = refs[n_in + n_r + n_out:n_in + 2 * n_r + n_out]
        scratch = refs[n_in + 2 * n_r + n_out:]
        step = pl.program_id(0)
        for axis in range(1, len(grid)):
            step = step * grid[axis] + pl.program_id(axis)
        for w_ref, o_ref, nb in zip(r_ins, r_outs, counts):
            @pl.when(step < nb)
            def _():
                for c in range(o_ref.shape[0]):
                    o_ref[c] = w_ref[:, c * MXU_COLS:(c + 1) * MXU_COLS].astype(BF16)
        body(*ins, *outs, *scratch)

    return kern


def _ffn_kernel(x_ref, g_ref, w1_ref, w3_ref, w2_ref, o_ref):
    o_ref[...] = _swiglu_half_step(x_ref[...], g_ref[...], w1_ref, w3_ref, w2_ref)


def _ffn_call(x2d, g, w1, w3, w2, riders, *, tm):
    t, d = x2d.shape
    grid = (t // tm,)
    r_in, r_out, r_shapes, counts = _rider_specs(riders, grid)
    return pl.pallas_call(
        _with_riders(_ffn_kernel, 5, 1, counts, grid),
        out_shape=[jax.ShapeDtypeStruct((t, d), F32)] + r_shapes,
        grid=grid,
        in_specs=[
            pl.BlockSpec((tm, d), lambda i: (i, 0)),
            _const_spec((1, d)),
            _const_spec(w1.shape),
            _const_spec(w3.shape),
            _const_spec(w2.shape),
        ] + r_in,
        out_specs=[pl.BlockSpec((tm, d), lambda i: (i, 0))] + r_out,
        compiler_params=pltpu.CompilerParams(
            dimension_semantics=("arbitrary",), vmem_limit_bytes=VMEM_LIMIT_BYTES),
        name="ffn1",
    )(x2d, g, w1, w3, w2, *riders)


def _retention_head(hd, q, k, v, g_ret, cosq, sinq, dmat_ref, qdec_ref, kdec_ref, cdec_ref,
                    gn_ref, state_ref):
    ts = q.shape[0]
    qs = slice(hd * RET_DK, (hd + 1) * RET_DK)
    vs = slice(hd * RET_DV, (hd + 1) * RET_DV)
    qh = q[:, qs]
    kh = k[:, qs]
    qh = qh * cosq + pltpu.roll(qh, RET_DK // 2, axis=1) * sinq
    kh = (kh * cosq + pltpu.roll(kh, RET_DK // 2, axis=1) * sinq) * (RET_DK ** -0.5)
    dmat = dmat_ref[hd]
    qdec = qdec_ref[hd]
    kdec = kdec_ref[hd]
    cdec = cdec_ref[hd]
    out = []
    for c in range(ts // RET_CHUNK):
        rows = slice(c * RET_CHUNK, (c + 1) * RET_CHUNK)
        qc = qh[rows]
        kc = kh[rows]
        vc = v[rows]
        st = state_ref[hd]
        scores = lax.dot_general(qc.astype(BF16), kc.astype(BF16), (((1,), (1,)), ((), ())),
                                 preferred_element_type=F32) * dmat
        inner = _dot(scores.astype(BF16), vc)
        cross = _dot((qc * qdec).astype(BF16), st.astype(BF16))
        kv = lax.dot_general((kc * kdec).astype(BF16), vc, (((0,), (0,)), ((), ())),
                             preferred_element_type=F32)
        state_ref[hd] = cdec * st + kv
        y = inner + cross
        mu = jnp.mean(y, axis=-1, keepdims=True)
        yc = y - mu
        var = jnp.mean(yc * yc, axis=-1, keepdims=True)
        yn = yc * lax.rsqrt(var + EPS) * gn_ref[:, vs]
        gt = g_ret[rows]
        out.append((gt * jax.nn.sigmoid(gt) * yn).astype(BF16))
    return jnp.concatenate(out, axis=0)


def _lru_lanes(blk, x_lru, g_lru, cw_ref, cb_ref, w_ri_ref, b_r_ref, b_i_ref, lam_ref,
               hcar_ref, prev_ref, cbuf_ref, hloc_ref, pcum_ref):
    ts, lw = x_lru.shape
    ls = slice(blk * lw, (blk + 1) * lw)
    lb = w_ri_ref.shape[1]
    seg = ts // SUBLANES
    hist = CONV_W - 1
    sub = lax.broadcasted_iota(jnp.int32, (SUBLANES, lw), 0)
    for i in range(hist):
        cur = x_lru[(seg - hist + i) * SUBLANES:(seg - hist + i + 1) * SUBLANES, :]
        prv = prev_ref[i * SUBLANES:(i + 1) * SUBLANES, ls]
        cbuf_ref[i * SUBLANES:(i + 1) * SUBLANES, ls] = jnp.where(
            sub == 0, pltpu.roll(prv, 1, axis=0), pltpu.roll(cur, 1, axis=0))
    cbuf_ref[hist * SUBLANES:, ls] = x_lru
    prev_ref[:, ls] = x_lru[(seg - hist) * SUBLANES:, :]
    xc = x_lru * cw_ref[CONV_W - 1:CONV_W, ls] + cb_ref[:, ls]
    for tap in range(CONV_W - 1):
        xc = xc + cbuf_ref[tap * SUBLANES:tap * SUBLANES + ts, ls] * cw_ref[tap:tap + 1, ls]

    xcb = xc.astype(BF16)
    per = lw // lb
    ri = [_dot(xcb[:, gb * lb:(gb + 1) * lb], w_ri_ref[blk * per + gb]) for gb in range(per)]
    r = jax.nn.sigmoid(jnp.concatenate([p[:, :lb] for p in ri], axis=1) + b_r_ref[:, ls])
    ig = jax.nn.sigmoid(jnp.concatenate([p[:, lb:] for p in ri], axis=1) + b_i_ref[:, ls])
    nlam = -lam_ref[:, ls]
    softplus = jnp.maximum(nlam, 0.0) + jnp.log1p(jnp.exp(-jnp.abs(nlam)))
    log_a = (-LRU_C * softplus) * r
    a = jnp.exp(log_a)
    th = jnp.tanh(log_a)
    mult = jnp.sqrt(-2.0 * th / (1.0 - th))
    bx = mult * (ig * xc)

    for j in range(seg):
        rows = slice(j * SUBLANES, (j + 1) * SUBLANES)
        if j == 0:
            hloc, pcum = bx[rows], a[rows]
        else:
            hloc, pcum = a[rows] * hloc + bx[rows], a[rows] * pcum
        hloc_ref[rows, ls] = hloc
        pcum_ref[rows, ls] = pcum
    start = jnp.broadcast_to(hcar_ref[:, ls], (SUBLANES, lw))
    for s in range(1, SUBLANES):
        start = jnp.where(sub == s, pltpu.roll(hloc + pcum * start, 1, axis=0), start)
    hcar_ref[:, ls] = (hloc + pcum * start)[SUBLANES - 1:SUBLANES, :]
    hl = (hloc_ref[:, ls].reshape(seg, SUBLANES, lw)
          + pcum_ref[:, ls].reshape(seg, SUBLANES, lw) * start[None]).reshape(ts, lw)
    return (hl * jax.nn.gelu(g_lru)).astype(BF16)


def _mixer_kernel(x_ref, g_ref, w_in_ref, w_bg_ref, b_bg_ref,
                  cosq_ref, sinq_ref, dmat_ref, qdec_ref, kdec_ref, cdec_ref,
                  gn_ref, w_ro_ref,
                  cw_ref, cb_ref, w_ri_ref, b_r_ref, b_i_ref, lam_ref, w_lo_ref,
                  w_out_ref, perm_ref, permt_ref, o_ref,
                  state_ref, hcar_ref, prev_ref, cbuf_ref, hloc_ref, pcum_ref, *, ts):
    d = x_ref.shape[-1]
    qw = RET_HEADS * RET_DK
    vw = RET_HEADS * RET_DV
    lw = lam_ref.shape[-1]
    widths = (qw, qw, vw, vw, lw, lw)
    o_q, o_k, o_v, o_g, o_x, o_gl, o_end = [sum(widths[:i]) // MXU_COLS for i in range(7)]
    n_blk = lw // MXU_COLS

    @pl.when(pl.program_id(1) == 0)
    def _():
        state_ref[...] = jnp.zeros_like(state_ref)
        hcar_ref[...] = jnp.zeros_like(hcar_ref)
        prev_ref[...] = jnp.zeros_like(prev_ref)

    x = x_ref[0]
    h = _rms(x, g_ref[...]).astype(BF16)
    subs = [slice(r, r + LRU_TILE) for r in range(0, ts, LRU_TILE)]
    hp = jnp.concatenate([_dot_cols(perm_ref[...], h[rs]).astype(BF16) for rs in subs], axis=0)
    q = _dot_slabs(h, w_in_ref, o_q, o_k)
    k = _dot_slabs(h, w_in_ref, o_k, o_v)
    cosq = cosq_ref[...]
    sinq = sinq_ref[...]

    gated, hg = [], []
    for i in range(max(RET_HEADS, n_blk)):
        if i < RET_HEADS:
            v = _dot(h, w_in_ref[o_v + i]).astype(BF16)
            g_ret = _dot(h, w_in_ref[o_g + i])
            gated.append(_retention_head(i, q, k, v, g_ret, cosq, sinq, dmat_ref, qdec_ref,
                                         kdec_ref, cdec_ref, gn_ref, state_ref))
        if i < n_blk:
            x_lru = _dot(hp, w_in_ref[o_x + i])
            g_lru = _dot(hp, w_in_ref[o_gl + i])
            hg.append(jnp.concatenate(
                [_lru_lanes(i, x_lru[rs], g_lru[rs], cw_ref, cb_ref, w_ri_ref, b_r_ref, b_i_ref,
                            lam_ref, hcar_ref, prev_ref, cbuf_ref, hloc_ref, pcum_ref)
                 for rs in subs], axis=0))
    y_ret = _dot_slabs(jnp.concatenate(gated, axis=1), w_ro_ref)
    hg = jnp.concatenate(hg, axis=1)
    hg = jnp.concatenate([_dot_cols(permt_ref[...], hg[rs]).astype(BF16) for rs in subs], axis=0)
    y_lru = _dot_slabs(hg, w_lo_ref)

    gates = jax.nn.sigmoid(_dot_slabs(h, w_bg_ref) + b_bg_ref[...])
    merged = gates[:, :d] * y_ret + gates[:, d:] * y_lru
    o_ref[0] = x + _dot_slabs(merged.astype(BF16), w_out_ref)


def _mixer_call(x, g, w_in, w_bg, b_bg, cosq, sinq, dmat, qdec, kdec, cdec, gn, w_ro,
                cw, cb, w_ri, b_r, b_i, lam, w_lo, w_out, riders, *, ts):
    b, s, d = x.shape
    grid = (b, s // ts)
    r_in, r_out, r_shapes, counts = _rider_specs(riders, grid)
    lw = lam.shape[-1]
    consts = (g, w_in, w_bg, b_bg)
    perm = _segment_permutation(LRU_TILE)
    consts2 = (dmat, qdec, kdec, cdec, gn, w_ro, cw, cb, w_ri, b_r, b_i, lam, w_lo, w_out,
               perm, perm.T)
    rope_spec = pl.BlockSpec((ts, RET_DK), lambda bi, si: (si, 0))
    n_in = 3 + len(consts) + len(consts2)
    return pl.pallas_call(
        _with_riders(functools.partial(_mixer_kernel, ts=ts), n_in, 1, counts, grid),
        out_shape=[jax.ShapeDtypeStruct((b, s, d), F32)] + r_shapes,
        grid=grid,
        in_specs=([pl.BlockSpec((1, ts, d), lambda bi, si: (bi, si, 0))]
                  + [_const_spec(c.shape) for c in consts]
                  + [rope_spec, rope_spec]
                  + [_const_spec(c.shape) for c in consts2] + r_in),
        out_specs=[pl.BlockSpec((1, ts, d), lambda bi, si: (bi, si, 0))] + r_out,
        scratch_shapes=[
            pltpu.VMEM((RET_HEADS, RET_DK, RET_DV), F32),
            pltpu.VMEM((1, lw), F32),
            pltpu.VMEM(((CONV_W - 1) * SUBLANES, lw), F32),
            pltpu.VMEM((LRU_TILE + (CONV_W - 1) * SUBLANES, lw), F32),
            pltpu.VMEM((LRU_TILE, lw), F32),
            pltpu.VMEM((LRU_TILE, lw), F32),
        ],
        compiler_params=pltpu.CompilerParams(
            dimension_semantics=("arbitrary", "arbitrary"), vmem_limit_bytes=VMEM_LIMIT_BYTES),
        name="mixer",
    )(x, *consts, cosq, sinq, *consts2, *riders)


def _memkv_kernel(m_ref, g_ref, wk_ref, wv_ref, k_ref, v_ref, wkb_ref, wvb_ref):
    hd = k_ref.shape[-1]

    @pl.when(pl.program_id(0) == 0)
    def _():
        for hh in range(X_HEADS):
            wkb_ref[hh] = wk_ref[:, hh * hd:(hh + 1) * hd].astype(BF16)
            wvb_ref[hh] = wv_ref[:, hh * hd:(hh + 1) * hd].astype(BF16)

    m = _rms(m_ref[0], g_ref[...]).astype(BF16)
    for hh in range(X_HEADS):
        k_ref[0, hh] = _dot(m, wkb_ref[hh]).astype(BF16)
        v_ref[0, hh] = _dot(m, wvb_ref[hh]).astype(BF16)


def _memkv_call(mem, g, wk, wv, riders):
    b, m, d = mem.shape
    hd = d // X_HEADS
    grid = (b,)
    r_in, r_out, r_shapes, counts = _rider_specs(riders, grid)
    kv_shape = jax.ShapeDtypeStruct((b, X_HEADS, m, hd), BF16)
    kv_spec = pl.BlockSpec((1, X_HEADS, m, hd), lambda i: (i, 0, 0, 0))
    return pl.pallas_call(
        _with_riders(_memkv_kernel, 4, 2, counts, grid),
        out_shape=[kv_shape, kv_shape] + r_shapes,
        grid=grid,
        in_specs=[pl.BlockSpec((1, m, d), lambda i: (i, 0, 0)), _const_spec((1, d)),
                  _const_spec(wk.shape), _const_spec(wv.shape)] + r_in,
        out_specs=[kv_spec, kv_spec] + r_out,
        scratch_shapes=[pltpu.VMEM((X_HEADS, d, hd), BF16), pltpu.VMEM((X_HEADS, d, hd), BF16)],
        compiler_params=pltpu.CompilerParams(
            dimension_semantics=("arbitrary",), vmem_limit_bytes=VMEM_LIMIT_BYTES),
        name="mem_kv",
    )(mem, g, wk, wv, *riders)


def _xattn_ffn_kernel(x_ref, gq_ref, wq_ref, k_ref, v_ref, wo_ref,
                      g2_ref, w1_ref, w3_ref, w2_ref, gf_ref, o_ref):
    x = x_ref[0]
    d = x.shape[-1]
    hd = d // X_HEADS
    hq = _rms(x, gq_ref[...]).astype(BF16)
    xq = [(_dot(hq, wq_ref[hh]) * (hd ** -0.5)).astype(BF16) for hh in range(X_HEADS)]
    xo = []
    for hh in range(X_HEADS):
        sc = lax.dot_general(xq[hh], k_ref[0, hh], (((1,), (1,)), ((), ())),
                             preferred_element_type=F32)
        e = jnp.exp(sc - jnp.max(sc, axis=-1, keepdims=True))
        pv = _dot(e.astype(BF16), v_ref[0, hh])
        xo.append((pv / jnp.sum(e, axis=-1, keepdims=True)).astype(BF16))
    x3 = x + _dot_slabs(jnp.concatenate(xo, axis=1), wo_ref)
    y = _swiglu_half_step(x3, g2_ref[...], w1_ref, w3_ref, w2_ref)
    o_ref[0] = _rms(y, gf_ref[...])


def _xattn_ffn_call(x, gq, wq, xk, xv, wo, g2, w1, w3, w2, gf, *, ts):
    b, s, d = x.shape
    kv_spec = pl.BlockSpec((1,) + xk.shape[1:], lambda bi, si: (bi, 0, 0, 0))
    return pl.pallas_call(
        _xattn_ffn_kernel,
        out_shape=jax.ShapeDtypeStruct((b, s, d), F32),
        grid=(b, s // ts),
        in_specs=[
            pl.BlockSpec((1, ts, d), lambda bi, si: (bi, si, 0)),
            _const_spec((1, d)),
            _const_spec(wq.shape),
            kv_spec,
            kv_spec,
            _const_spec(wo.shape),
            _const_spec((1, d)),
            _const_spec(w1.shape),
            _const_spec(w3.shape),
            _const_spec(w2.shape),
            _const_spec((1, d)),
        ],
        out_specs=pl.BlockSpec((1, ts, d), lambda bi, si: (bi, si, 0)),
        compiler_params=pltpu.CompilerParams(
            dimension_semantics=("arbitrary", "arbitrary"), vmem_limit_bytes=VMEM_LIMIT_BYTES),
        name="xattn_ffn2",
    )(x, gq, wq, xk, xv, wo, g2, w1, w3, w2, gf)


def _rope_tables(s):
    pos = jnp.arange(s, dtype=F32)
    inv_freq = ROPE_BASE ** (-jnp.arange(0, RET_DK, 2, dtype=F32) / RET_DK)
    ang = pos[:, None] * inv_freq[None, :]
    cos, sin = jnp.cos(ang), jnp.sin(ang)
    return jnp.concatenate([cos, cos], axis=-1), jnp.concatenate([-sin, sin], axis=-1)


def _segment_permutation(ts):
    seg = ts // SUBLANES
    r = jnp.arange(ts)
    src = (r % SUBLANES) * seg + r // SUBLANES
    return (src[:, None] == jnp.arange(ts)[None, :]).astype(BF16)


def _decay_tables():
    log_gamma = jnp.log(1.0 - 2.0 ** (-5.0 - jnp.arange(RET_HEADS, dtype=F32)))
    lg = log_gamma[:, None, None]
    pos = jnp.arange(RET_CHUNK, dtype=F32)
    rel = pos[:, None] - pos[None, :]
    dmat = jnp.where(rel[None] >= 0, jnp.exp(rel[None] * lg), 0.0)
    ones = jnp.ones((1, RET_CHUNK, RET_DK), F32)
    qdec = jnp.exp((pos + 1.0)[None, :, None] * lg) * ones
    kdec = jnp.exp((RET_CHUNK - 1.0 - pos)[None, :, None] * lg) * ones
    cdec = jnp.exp(RET_CHUNK * lg) * jnp.ones((1, RET_DK, RET_DV), F32)
    return dmat, qdec, kdec, cdec


def kernel(x, mem, ffn1_norm, ffn1_w1, ffn1_w3, ffn1_w2, mix_norm, w_in, ret_gn, w_ret_o, conv_w, conv_b, w_rgate, b_rgate, w_igate, b_igate, lru_lambda, w_lru_o, w_branch_gate, b_branch_gate, w_out, xattn_norm, mem_norm, w_xq, w_xk, w_xv, w_xo, ffn2_norm, ffn2_w1, ffn2_w3, ffn2_w2, final_norm):
    b, s, d = x.shape
    depth = ffn1_norm.shape[0]
    tm, ts, ts_x = 512, 512, 512
    row = lambda p: p.reshape(1, -1)
    cosq, sinq = _rope_tables(s)
    dmat, qdec, kdec, cdec = _decay_tables()
    assert depth == 1
    l = 0
    xk, xv, f1_w1, f1_w3, f1_w2 = _memkv_call(
        mem, row(mem_norm[l]), w_xk[l], w_xv[l], [ffn1_w1[l], ffn1_w3[l], ffn1_w2[l]])
    x, m_in, m_bg, m_ro, m_lo, m_out = _ffn_call(
        x.reshape(b * s, d), row(ffn1_norm[l]), f1_w1, f1_w3, f1_w2,
        [w_in[l], w_branch_gate[l], w_ret_o[l], w_lru_o[l], w_out[l]], tm=tm)
    w_ri = jnp.concatenate([w_rgate[l], w_igate[l]], axis=-1).astype(BF16)
    x, x_wq, x_wo, f2_w1, f2_w3, f2_w2 = _mixer_call(
        x.reshape(b, s, d), row(mix_norm[l]), m_in, m_bg, row(b_branch_gate[l]),
        cosq, sinq, dmat, qdec, kdec, cdec, row(ret_gn[l]), m_ro,
        conv_w[l], row(conv_b[l]), w_ri, row(b_rgate[l]), row(b_igate[l]),
        row(lru_lambda[l]), m_lo, m_out,
        [w_xq[l], w_xo[l], ffn2_w1[l], ffn2_w3[l], ffn2_w2[l]], ts=ts)
    return _xattn_ffn_call(x, row(xattn_norm[l]), x_wq, xk, xv, x_wo, row(ffn2_norm[l]),
                           f2_w1, f2_w3, f2_w2, row(final_norm), ts=ts_x)
```

```python
import functools

import jax
import jax.numpy as jnp
from jax import lax
from jax.experimental import pallas as pl
from jax.experimental.pallas import tpu as pltpu

EPS = 1e-6
RET_HEADS = 4
RET_DK = 128
RET_DV = 256
RET_CHUNK = 128
ROPE_BASE = 10000.0
LRU_BLOCKS = 8
CONV_W = 4
LRU_C = 8.0
LRU_TILE = 256
X_HEADS = 4

SUBLANES = 8
BF16_TILE_ROWS = 16
MXU_COLS = 256
VMEM_LIMIT_BYTES = 56 * 1024 * 1024

BF16 = jnp.bfloat16
F32 = jnp.float32


def _dot(a, b):
    return jnp.dot(a, b, preferred_element_type=F32)


def _dot_slabs(a, w_ref, lo=0, hi=None, rows=slice(None)):
    hi = w_ref.shape[0] if hi is None else hi
    parts = [_dot(a, w_ref[c, rows, :]) for c in range(lo, hi)]
    return parts[0] if len(parts) == 1 else jnp.concatenate(parts, axis=1)


def _dot_cols(a, b):
    parts = [_dot(a, b[:, c:c + MXU_COLS]) for c in range(0, b.shape[1], MXU_COLS)]
    return parts[0] if len(parts) == 1 else jnp.concatenate(parts, axis=1)


def _rms(x, g):
    ms = jnp.mean(x * x, axis=-1, keepdims=True)
    return x * lax.rsqrt(ms + EPS) * g


def _const_spec(shape):
    nd = len(shape)
    return pl.BlockSpec(shape, lambda *_: (0,) * nd, pipeline_mode=pl.Buffered(1))


def _swiglu_half_step(x, g, w1_ref, w3_ref, w2_ref):
    h = _rms(x, g).astype(BF16)
    acc = None
    for j in range(w1_ref.shape[0]):
        a = _dot(h, w1_ref[j])
        b = _dot(h, w3_ref[j])
        hid = (a * jax.nn.sigmoid(a) * b).astype(BF16)
        part = _dot_slabs(hid, w2_ref, rows=slice(j * MXU_COLS, (j + 1) * MXU_COLS))
        acc = part if acc is None else acc + part
    return x + 0.5 * acc


def _rider_blocks(w, n_steps):
    kdim = w.shape[0]
    return max(n for n in range(1, n_steps + 1)
               if kdim % n == 0 and (kdim // n) % BF16_TILE_ROWS == 0)


def _rider_specs(weights, grid):
    n_steps = 1
    for g in grid:
        n_steps *= g
    in_specs, out_specs, out_shapes, counts = [], [], [], []
    for w in weights:
        kdim, n = w.shape
        nb = _rider_blocks(w, n_steps)

        def block(*ids, nb=nb):
            lin = ids[0]
            for i, g in zip(ids[1:], grid[1:]):
                lin = lin * g + i
            return jnp.minimum(lin, nb - 1)

        in_specs.append(pl.BlockSpec((kdim // nb, n), lambda *ids, block=block: (block(*ids), 0)))
        out_specs.append(pl.BlockSpec((n // MXU_COLS, kdim // nb, MXU_COLS),
                                      lambda *ids, block=block: (0, block(*ids), 0)))
        out_shapes.append(jax.ShapeDtypeStruct((n // MXU_COLS, kdim, MXU_COLS), BF16))
        counts.append(nb)
    return in_specs, out_specs, out_shapes, counts


def _with_riders(body, n_in, n_out, counts, grid):
    n_r = len(counts)

    def kern(*refs):
        ins, r_ins = refs[:n_in], refs[n_in:n_in + n_r]
        outs = refs[n_in + n_r:n_in + n_r + n_out]
        r_outs = refs[n_in + n_r + n_out:n_in + 2 * n_r + n_out]
        scratch = refs[n_in + 2 * n_r + n_out:]
        step = pl.program_id(0)
        for axis in range(1, len(grid)):
            step = step * grid[axis] + pl.program_id(axis)
        for w_ref, o_ref, nb in zip(r_ins, r_outs, counts):
            @pl.when(step < nb)
            def _():
                for c in range(o_ref.shape[0]):
                    o_ref[c] = w_ref[:, c * MXU_COLS:(c + 1) * MXU_COLS].astype(BF16)
        body(*ins, *outs, *scratch)

    return kern


def _ffn_kernel(x_ref, g_ref, w1_ref, w3_ref, w2_ref, o_ref):
    o_ref[...] = _swiglu_half_step(x_ref[...], g_ref[...], w1_ref, w3_ref, w2_ref)


def _ffn_call(x2d, g, w1, w3, w2, riders, *, tm):
    t, d = x2d.shape
    grid = (t // tm,)
    r_in, r_out, r_shapes, counts = _rider_specs(riders, grid)
    return pl.pallas_call(
        _with_riders(_ffn_kernel, 5, 1, counts, grid),
        out_shape=[jax.ShapeDtypeStruct((t, d), F32)] + r_shapes,
        grid=grid,
        in_specs=[
            pl.BlockSpec((tm, d), lambda i: (i, 0)),
            _const_spec((1, d)),
            _const_spec(w1.shape),
            _const_spec(w3.shape),
            _const_spec(w2.shape),
        ] + r_in,
        out_specs=[pl.BlockSpec((tm, d), lambda i: (i, 0))] + r_out,
        compiler_params=pltpu.CompilerParams(
            dimension_semantics=("arbitrary",), vmem_limit_bytes=VMEM_LIMIT_BYTES),
        name="ffn1",
    )(x2d, g, w1, w3, w2, *riders)


def _retention_head(hd, q, k, v, g_ret, cosq, sinq, dmat_ref, qdec_ref, kdec_ref, cdec_ref,
                    gn_ref, state_ref):
    ts = q.shape[0]
    qs = slice(hd * RET_DK, (hd + 1) * RET_DK)
    vs = slice(hd * RET_DV, (hd + 1) * RET_DV)
    qh = q[:, qs]
    kh = k[:, qs]
    qh = qh * cosq + pltpu.roll(qh, RET_DK // 2, axis=1) * sinq
    kh = (kh * cosq + pltpu.roll(kh, RET_DK // 2, axis=1) * sinq) * (RET_DK ** -0.5)
    dmat = dmat_ref[hd]
    qdec = qdec_ref[hd]
    kdec = kdec_ref[hd]
    cdec = cdec_ref[hd]
    out = []
    for c in range(ts // RET_CHUNK):
        rows = slice(c * RET_CHUNK, (c + 1) * RET_CHUNK)
        qc = qh[rows]
        kc = kh[rows]
        vc = v[rows]
        st = state_ref[hd]
        scores = lax.dot_general(qc.astype(BF16), kc.astype(BF16), (((1,), (1,)), ((), ())),
                                 preferred_element_type=F32) * dmat
        inner = _dot(scores.astype(BF16), vc)
        cross = _dot((qc * qdec).astype(BF16), st.astype(BF16))
        kv = lax.dot_general((kc * kdec).astype(BF16), vc, (((0,), (0,)), ((), ())),
                             preferred_element_type=F32)
        state_ref[hd] = cdec * st + kv
        y = inner + cross
        mu = jnp.mean(y, axis=-1, keepdims=True)
        yc = y - mu
        var = jnp.mean(yc * yc, axis=-1, keepdims=True)
        yn = yc * lax.rsqrt(var + EPS) * gn_ref[:, vs]
        gt = g_ret[rows]
        out.append((gt * jax.nn.sigmoid(gt) * yn).astype(BF16))
    return jnp.concatenate(out, axis=0)


def _lru_lanes(blk, x_lru, g_lru, cw_ref, cb_ref, w_ri_ref, b_r_ref, b_i_ref, lam_ref,
               hcar_ref, prev_ref, cbuf_ref, hloc_ref, pcum_ref):
    ts, lw = x_lru.shape
    ls = slice(blk * lw, (blk + 1) * lw)
    lb = w_ri_ref.shape[1]
    seg = ts // SUBLANES
    hist = CONV_W - 1
    sub = lax.broadcasted_iota(jnp.int32, (SUBLANES, lw), 0)
    for i in range(hist):
        cur = x_lru[(seg - hist + i) * SUBLANES:(seg - hist + i + 1) * SUBLANES, :]
        prv = prev_ref[i * SUBLANES:(i + 1) * SUBLANES, ls]
        cbuf_ref[i * SUBLANES:(i + 1) * SUBLANES, ls] = jnp.where(
            sub == 0, pltpu.roll(prv, 1, axis=0), pltpu.roll(cur, 1, axis=0))
    cbuf_ref[hist * SUBLANES:, ls] = x_lru
    prev_ref[:, ls] = x_lru[(seg - hist) * SUBLANES:, :]
    xc = x_lru * cw_ref[CONV_W - 1:CONV_W, ls] + cb_ref[:, ls]
    for tap in range(CONV_W - 1):
        xc = xc + cbuf_ref[tap * SUBLANES:tap * SUBLANES + ts, ls] * cw_ref[tap:tap + 1, ls]

    xcb = xc.astype(BF16)
    per = lw // lb
    ri = [_dot(xcb[:, gb * lb:(gb + 1) * lb], w_ri_ref[blk * per + gb]) for gb in range(per)]
    r = jax.nn.sigmoid(jnp.concatenate([p[:, :lb] for p in ri], axis=1) + b_r_ref[:, ls])
    ig = jax.nn.sigmoid(jnp.concatenate([p[:, lb:] for p in ri], axis=1) + b_i_ref[:, ls])
    nlam = -lam_ref[:, ls]
    softplus = jnp.maximum(nlam, 0.0) + jnp.log1p(jnp.exp(-jnp.abs(nlam)))
    log_a = (-LRU_C * softplus) * r
    a = jnp.exp(log_a)
    th = jnp.tanh(log_a)
    mult = jnp.sqrt(-2.0 * th / (1.0 - th))
    bx = mult * (ig * xc)

    for j in range(seg):
        rows = slice(j * SUBLANES, (j + 1) * SUBLANES)
        if j == 0:
            hloc, pcum = bx[rows], a[rows]
        else:
            hloc, pcum = a[rows] * hloc + bx[rows], a[rows] * pcum
        hloc_ref[rows, ls] = hloc
        pcum_ref[rows, ls] = pcum
    start = jnp.broadcast_to(hcar_ref[:, ls], (SUBLANES, lw))
    for s in range(1, SUBLANES):
        start = jnp.where(sub == s, pltpu.roll(hloc + pcum * start, 1, axis=0), start)
    hcar_ref[:, ls] = (hloc + pcum * start)[SUBLANES - 1:SUBLANES, :]
    hl = (hloc_ref[:, ls].reshape(seg, SUBLANES, lw)
          + pcum_ref[:, ls].reshape(seg, SUBLANES, lw) * start[None]).reshape(ts, lw)
    return (hl * jax.nn.gelu(g_lru)).astype(BF16)


def _mixer_kernel(x_ref, g_ref, w_in_ref, w_bg_ref, b_bg_ref,
                  cosq_ref, sinq_ref, dmat_ref, qdec_ref, kdec_ref, cdec_ref,
                  gn_ref, w_ro_ref,
                  cw_ref, cb_ref, w_ri_ref, b_r_ref, b_i_ref, lam_ref, w_lo_ref,
                  w_out_ref, perm_ref, permt_ref, o_ref,
                  state_ref, hcar_ref, prev_ref, cbuf_ref, hloc_ref, pcum_ref, *, ts):
    d = x_ref.shape[-1]
    qw = RET_HEADS * RET_DK
    vw = RET_HEADS * RET_DV
    lw = lam_ref.shape[-1]
    widths = (qw, qw, vw, vw, lw, lw)
    o_q, o_k, o_v, o_g, o_x, o_gl, o_end = [sum(widths[:i]) // MXU_COLS for i in range(7)]
    n_blk = lw // MXU_COLS

    @pl.when(pl.program_id(1) == 0)
    def _():
        state_ref[...] = jnp.zeros_like(state_ref)
        hcar_ref[...] = jnp.zeros_like(hcar_ref)
        prev_ref[...] = jnp.zeros_like(prev_ref)

    x = x_ref[0]
    h = _rms(x, g_ref[...]).astype(BF16)
    subs = [slice(r, r + LRU_TILE) for r in range(0, ts, LRU_TILE)]
    hp = jnp.concatenate([_dot_cols(perm_ref[...], h[rs]).astype(BF16) for rs in subs], axis=0)
    q = _dot_slabs(h, w_in_ref, o_q, o_k)
    k = _dot_slabs(h, w_in_ref, o_k, o_v)
    cosq = cosq_ref[...]
    sinq = sinq_ref[...]

    gated, hg = [], []
    for i in range(max(RET_HEADS, n_blk)):
        if i < RET_HEADS:
            v = _dot(h, w_in_ref[o_v + i]).astype(BF16)
            g_ret = _dot(h, w_in_ref[o_g + i])
            gated.append(_retention_head(i, q, k, v, g_ret, cosq, sinq, dmat_ref, qdec_ref,
                                         kdec_ref, cdec_ref, gn_ref, state_ref))
        if i < n_blk:
            x_lru = _dot(hp, w_in_ref[o_x + i])
            g_lru = _dot(hp, w_in_ref[o_gl + i])
            hg.append(jnp.concatenate(
                [_lru_lanes(i, x_lru[rs], g_lru[rs], cw_ref, cb_ref, w_ri_ref, b_r_ref, b_i_ref,
                            lam_ref, hcar_ref, prev_ref, cbuf_ref, hloc_ref, pcum_ref)
                 for rs in subs], axis=0))
    y_ret = _dot_slabs(jnp.concatenate(gated, axis=1), w_ro_ref)
    hg = jnp.concatenate(hg, axis=1)
    hg = jnp.concatenate([_dot_cols(permt_ref[...], hg[rs]).astype(BF16) for rs in subs], axis=0)
    y_lru = _dot_slabs(hg, w_lo_ref)

    gates = jax.nn.sigmoid(_dot_slabs(h, w_bg_ref) + b_bg_ref[...])
    merged = gates[:, :d] * y_ret + gates[:, d:] * y_lru
    o_ref[0] = x + _dot_slabs(merged.astype(BF16), w_out_ref)


def _mixer_call(x, g, w_in, w_bg, b_bg, cosq, sinq, dmat, qdec, kdec, cdec, gn, w_ro,
                cw, cb, w_ri, b_r, b_i, lam, w_lo, w_out, riders, *, ts):
    b, s, d = x.shape
    grid = (b, s // ts)
    r_in, r_out, r_shapes, counts = _rider_specs(riders, grid)
    lw = lam.shape[-1]
    consts = (g, w_in, w_bg, b_bg)
    perm = _segment_permutation(LRU_TILE)
    consts2 = (dmat, qdec, kdec, cdec, gn, w_ro, cw, cb, w_ri, b_r, b_i, lam, w_lo, w_out,
               perm, perm.T)
    rope_spec = pl.BlockSpec((ts, RET_DK), lambda bi, si: (si, 0))
    n_in = 3 + len(consts) + len(consts2)
    return pl.pallas_call(
        _with_riders(functools.partial(_mixer_kernel, ts=ts), n_in, 1, counts, grid),
        out_shape=[jax.ShapeDtypeStruct((b, s, d), F32)] + r_shapes,
        grid=grid,
        in_specs=([pl.BlockSpec((1, ts, d), lambda bi, si: (bi, si, 0))]
                  + [_const_spec(c.shape) for c in consts]
                  + [rope_spec, rope_spec]
                  + [_const_spec(c.shape) for c in consts2] + r_in),
        out_specs=[pl.BlockSpec((1, ts, d), lambda bi, si: (bi, si, 0))] + r_out,
        scratch_shapes=[
            pltpu.VMEM((RET_HEADS, RET_DK, RET_DV), F32),
            pltpu.VMEM((1, lw), F32),
            pltpu.VMEM(((CONV_W - 1) * SUBLANES, lw), F32),
            pltpu.VMEM((LRU_TILE + (CONV_W - 1) * SUBLANES, lw), F32),
            pltpu.VMEM((LRU_TILE, lw), F32),
            pltpu.VMEM((LRU_TILE, lw), F32),
        ],
        compiler_params=pltpu.CompilerParams(
            dimension_semantics=("arbitrary", "arbitrary"), vmem_limit_bytes=VMEM_LIMIT_BYTES),
        name="mixer",
    )(x, *consts, cosq, sinq, *consts2, *riders)


def _memkv_kernel(m_ref, g_ref, wk_ref, wv_ref, k_ref, v_ref, wkb_ref, wvb_ref):
    hd = k_ref.shape[-1]

    @pl.when(pl.program_id(0) == 0)
    def _():
        for hh in range(X_HEADS):
            wkb_ref[hh] = wk_ref[:, hh * hd:(hh + 1) * hd].astype(BF16)
            wvb_ref[hh] = wv_ref[:, hh * hd:(hh + 1) * hd].astype(BF16)

    m = _rms(m_ref[0], g_ref[...]).astype(BF16)
    for hh in range(X_HEADS):
        k_ref[0, hh] = _dot(m, wkb_ref[hh]).astype(BF16)
        v_ref[0, hh] = _dot(m, wvb_ref[hh]).astype(BF16)


def _memkv_call(mem, g, wk, wv, riders):
    b, m, d = mem.shape
    hd = d // X_HEADS
    grid = (b,)
    r_in, r_out, r_shapes, counts = _rider_specs(riders, grid)
    kv_shape = jax.ShapeDtypeStruct((b, X_HEADS, m, hd), BF16)
    kv_spec = pl.BlockSpec((1, X_HEADS, m, hd), lambda i: (i, 0, 0, 0))
    return pl.pallas_call(
        _with_riders(_memkv_kernel, 4, 2, counts, grid),
        out_shape=[kv_shape, kv_shape] + r_shapes,
        grid=grid,
        in_specs=[pl.BlockSpec((1, m, d), lambda i: (i, 0, 0)), _const_spec((1, d)),
                  _const_spec(wk.shape), _const_spec(wv.shape)] + r_in,
        out_specs=[kv_spec, kv_spec] + r_out,
        scratch_shapes=[pltpu.VMEM((X_HEADS, d, hd), BF16), pltpu.VMEM((X_HEADS, d, hd), BF16)],
        compiler_params=pltpu.CompilerParams(
            dimension_semantics=("arbitrary",), vmem_limit_bytes=VMEM_LIMIT_BYTES),
        name="mem_kv",
    )(mem, g, wk, wv, *riders)


def _xattn_ffn_kernel(x_ref, gq_ref, wq_ref, k_ref, v_ref, wo_ref,
                      g2_ref, w1_ref, w3_ref, w2_ref, gf_ref, o_ref):
    x = x_ref[0]
    d = x.shape[-1]
    hd = d // X_HEADS
    hq = _rms(x, gq_ref[...]).astype(BF16)
    xq = [(_dot(hq, wq_ref[hh]) * (hd ** -0.5)).astype(BF16) for hh in range(X_HEADS)]
    xo = []
    for hh in range(X_HEADS):
        sc = lax.dot_general(xq[hh], k_ref[0, hh], (((1,), (1,)), ((), ())),
                             preferred_element_type=F32)
        e = jnp.exp(sc - jnp.max(sc, axis=-1, keepdims=True))
        pv = _dot(e.astype(BF16), v_ref[0, hh])
        xo.append((pv / jnp.sum(e, axis=-1, keepdims=True)).astype(BF16))
    x3 = x + _dot_slabs(jnp.concatenate(xo, axis=1), wo_ref)
    y = _swiglu_half_step(x3, g2_ref[...], w1_ref, w3_ref, w2_ref)
    o_ref[0] = _rms(y, gf_ref[...])


def _xattn_ffn_call(x, gq, wq, xk, xv, wo, g2, w1, w3, w2, gf, *, ts):
    b, s, d = x.shape
    kv_spec = pl.BlockSpec((1,) + xk.shape[1:], lambda bi, si: (bi, 0, 0, 0))
    return pl.pallas_call(
        _xattn_ffn_kernel,
        out_shape=jax.ShapeDtypeStruct((b, s, d), F32),
        grid=(b, s // ts),
        in_specs=[
            pl.BlockSpec((1, ts, d), lambda bi, si: (bi, si, 0)),
            _const_spec((1, d)),
            _const_spec(wq.shape),
            kv_spec,
            kv_spec,
            _const_spec(wo.shape),
            _const_spec((1, d)),
            _const_spec(w1.shape),
            _const_spec(w3.shape),
            _const_spec(w2.shape),
            _const_spec((1, d)),
        ],
        out_specs=pl.BlockSpec((1, ts, d), lambda bi, si: (bi, si, 0)),
        compiler_params=pltpu.CompilerParams(
            dimension_semantics=("arbitrary", "arbitrary"), vmem_limit_bytes=VMEM_LIMIT_BYTES),
        name="xattn_ffn2",
    )(x, gq, wq, xk, xv, wo, g2, w1, w3, w2, gf)


def _rope_tables(s):
    pos = jnp.arange(s, dtype=F32)
    inv_freq = ROPE_BASE ** (-jnp.arange(0, RET_DK, 2, dtype=F32) / RET_DK)
    ang = pos[:, None] * inv_freq[None, :]
    cos, sin = jnp.cos(ang), jnp.sin(ang)
    return jnp.concatenate([cos, cos], axis=-1), jnp.concatenate([-sin, sin], axis=-1)


def _segment_permutation(ts):
    seg = ts // SUBLANES
    r = jnp.arange(ts)
    src = (r % SUBLANES) * seg + r // SUBLANES
    return (src[:, None] == jnp.arange(ts)[None, :]).astype(BF16)


def _decay_tables():
    log_gamma = jnp.log(1.0 - 2.0 ** (-5.0 - jnp.arange(RET_HEADS, dtype=F32)))
    lg = log_gamma[:, None, None]
    pos = jnp.arange(RET_CHUNK, dtype=F32)
    rel = pos[:, None] - pos[None, :]
    dmat = jnp.where(rel[None] >= 0, jnp.exp(rel[None] * lg), 0.0)
    ones = jnp.ones((1, RET_CHUNK, RET_DK), F32)
    qdec = jnp.exp((pos + 1.0)[None, :, None] * lg) * ones
    kdec = jnp.exp((RET_CHUNK - 1.0 - pos)[None, :, None] * lg) * ones
    cdec = jnp.exp(RET_CHUNK * lg) * jnp.ones((1, RET_DK, RET_DV), F32)
    return dmat, qdec, kdec, cdec


def kernel(x, mem, ffn1_norm, ffn1_w1, ffn1_w3, ffn1_w2, mix_norm, w_in, ret_gn, w_ret_o, conv_w, conv_b, w_rgate, b_rgate, w_igate, b_igate, lru_lambda, w_lru_o, w_branch_gate, b_branch_gate, w_out, xattn_norm, mem_norm, w_xq, w_xk, w_xv, w_xo, ffn2_norm, ffn2_w1, ffn2_w3, ffn2_w2, final_norm):
    b, s, d = x.shape
    depth = ffn1_norm.shape[0]
    tm, ts, ts_x = 1024, 512, 1024
    row = lambda p: p.reshape(1, -1)
    cosq, sinq = _rope_tables(s)
    dmat, qdec, kdec, cdec = _decay_tables()
    assert depth == 1
    l = 0
    xk, xv, f1_w1, f1_w3, f1_w2 = _memkv_call(
        mem, row(mem_norm[l]), w_xk[l], w_xv[l], [ffn1_w1[l], ffn1_w3[l], ffn1_w2[l]])
    x, m_in, m_bg, m_ro, m_lo, m_out = _ffn_call(
        x.reshape(b * s, d), row(ffn1_norm[l]), f1_w1, f1_w3, f1_w2,
        [w_in[l], w_branch_gate[l], w_ret_o[l], w_lru_o[l], w_out[l]], tm=tm)
    w_ri = jnp.concatenate([w_rgate[l], w_igate[l]], axis=-1).astype(BF16)
    x, x_wq, x_wo, f2_w1, f2_w3, f2_w2 = _mixer_call(
        x.reshape(b, s, d), row(mix_norm[l]), m_in, m_bg, row(b_branch_gate[l]),
        cosq, sinq, dmat, qdec, kdec, cdec, row(ret_gn[l]), m_ro,
        conv_w[l], row(conv_b[l]), w_ri, row(b_rgate[l]), row(b_igate[l]),
        row(lru_lambda[l]), m_lo, m_out,
        [w_xq[l], w_xo[l], ffn2_w1[l], ffn2_w3[l], ffn2_w2[l]], ts=ts)
    return _xattn_ffn_call(x, row(xattn_norm[l]), x_wq, xk, xv, x_wo, row(ffn2_norm[l]),
                           f2_w1, f2_w3, f2_w2, row(final_norm), ts=ts_x)
```
